```python
import jax, jax.numpy as jnp
from jax import lax
import numpy as np


D_MODEL = 2048
BATCH = 8
SEQ = 2048
DEPTH = 1

MIX_WIDTH = D_MODEL
GROUP_DIM = 128
FOURIER_WIDTH = MIX_WIDTH // 2
GMLP_WIDTH = MIX_WIDTH - FOURIER_WIDTH
N_FOURIER_GROUPS = FOURIER_WIDTH // GROUP_DIM
N_GMLP_HEADS = GMLP_WIDTH // GROUP_DIM
IN_PROJ_WIDTH = FOURIER_WIDTH + 2 * GMLP_WIDTH
CHUNK = 128
D_FF = 4 * D_MODEL
EPS = 1e-6

kernel_name = "hybrid_fourier_gmlp_encoder_block"


def rmsnorm(x, g):
    xf = x.astype(jnp.float32)
    y = xf * lax.rsqrt(jnp.mean(xf * xf, axis=-1, keepdims=True) + EPS)
    return (y * g.astype(jnp.float32)).astype(x.dtype)


def fourier_groups(a, w_f):
    B, S, _ = a.shape
    a4 = a.reshape(B, S, N_FOURIER_GROUPS, GROUP_DIM).astype(jnp.float32)
    f = jnp.real(jnp.fft.fft2(a4, axes=(1, 3), norm="ortho")).astype(a.dtype)
    y = jnp.einsum('bsgc,gcd->bsgd', f, w_f)
    return y.reshape(B, S, FOURIER_WIDTH)


def gmlp_groups(z, g_v, w_s, b_s):
    B, S, _ = z.shape
    z = jax.nn.gelu(z, approximate=False)
    u, v = z[..., :GMLP_WIDTH], z[..., GMLP_WIDTH:]
    v = rmsnorm(v.reshape(B, S, N_GMLP_HEADS, GROUP_DIM), g_v)
    v = v.reshape(B, S // CHUNK, CHUNK, N_GMLP_HEADS, GROUP_DIM)
    s = jnp.einsum('hpq,bnqhd->bnphd', w_s, v) + b_s.T[None, None, :, :, None]
    return u * s.reshape(B, S, GMLP_WIDTH)


def setup_inputs(seed: int = 0) -> dict:
    key = jax.random.key(seed)
    ks = jax.random.split(key, 14)
    f32 = jnp.float32
    x = jax.random.normal(ks[0], (BATCH, SEQ, D_MODEL), f32)
    norm_mix_g = 1.0 + 0.05 * jax.random.normal(ks[1], (D_MODEL,), f32)
    w_in = jax.random.normal(ks[2], (D_MODEL, IN_PROJ_WIDTH), f32) * D_MODEL ** -0.5
    fourier_w = jax.random.normal(ks[3], (N_FOURIER_GROUPS, GROUP_DIM, GROUP_DIM), f32) * GROUP_DIM ** -0.5
    gmlp_v_g = 1.0 + 0.05 * jax.random.normal(ks[4], (N_GMLP_HEADS, GROUP_DIM), f32)
    gmlp_ws = jax.random.normal(ks[5], (N_GMLP_HEADS, CHUNK, CHUNK), f32) * CHUNK ** -0.5
    gmlp_b = 1.0 + 0.01 * jax.random.normal(ks[6], (N_GMLP_HEADS, CHUNK), f32)
    w_out = jax.random.normal(ks[7], (MIX_WIDTH, D_MODEL), f32) * MIX_WIDTH ** -0.5
    norm_mlp_g = 1.0 + 0.05 * jax.random.normal(ks[8], (D_MODEL,), f32)
    w_up = jax.random.normal(ks[9], (D_MODEL, D_FF), f32) * D_MODEL ** -0.5
    w_down = jax.random.normal(ks[10], (D_FF, D_MODEL), f32) * D_FF ** -0.5
    norm_final_g = 1.0 + 0.05 * jax.random.normal(ks[11], (D_MODEL,), f32)
    return {"x": x, "norm_mix_g": norm_mix_g, "w_in": w_in, "fourier_w": fourier_w,
            "gmlp_v_g": gmlp_v_g, "gmlp_ws": gmlp_ws, "gmlp_b": gmlp_b, "w_out": w_out,
            "norm_mlp_g": norm_mlp_g, "w_up": w_up, "w_down": w_down,
            "norm_final_g": norm_final_g}


def reference(x, norm_mix_g, w_in, fourier_w, gmlp_v_g, gmlp_ws, gmlp_b, w_out,
              norm_mlp_g, w_up, w_down, norm_final_g):
    h = x
    for _ in range(DEPTH):
        p = jnp.einsum('bsd,de->bse', rmsnorm(h, norm_mix_g), w_in)
        y_f = fourier_groups(p[..., :FOURIER_WIDTH], fourier_w)
        y_g = gmlp_groups(p[..., FOURIER_WIDTH:], gmlp_v_g, gmlp_ws, gmlp_b)
        mix = jnp.concatenate([y_f, y_g], axis=-1)
        h = h + jnp.einsum('bse,ed->bsd', mix, w_out)
        a = jnp.einsum('bsd,df->bsf', rmsnorm(h, norm_mlp_g), w_up)
        h = h + jnp.einsum('bsf,fd->bsd', jnp.square(jax.nn.relu(a)), w_down)
    return rmsnorm(h, norm_final_g)
```

```python
import functools

import numpy as np
import jax
import jax.numpy as jnp
from jax import lax
from jax.experimental import pallas as pl
from jax.experimental.pallas import tpu as pltpu

D_MODEL = 2048
SEQ = 2048
GROUP_DIM = 128
FOURIER_WIDTH = 1024
GMLP_WIDTH = 1024
N_GROUPS = FOURIER_WIDTH // GROUP_DIM
N_HEADS = GMLP_WIDTH // GROUP_DIM
IN_PROJ_WIDTH = FOURIER_WIDTH + 2 * GMLP_WIDTH
CHUNK = 128
D_FF = 4 * D_MODEL
EPS = 1e-6

F32 = jnp.float32
BF16 = jnp.bfloat16

VMEM_LIMIT_BYTES = 56 * 1024 * 1024

TM_IN = 512
TK_FOURIER = 512
TM_OUT = 512
TM_MLP = 1024
TF_MLP = 512


def _dft_tables(n):
    k = np.arange(n, dtype=np.int64)
    ang = 2.0 * np.pi * ((k[:, None] * k[None, :]) % n).astype(np.float64) / n
    return np.cos(ang).astype(np.float32), np.sin(ang).astype(np.float32)


def _dot(a, b):
    return jnp.dot(a, b, preferred_element_type=F32)


def _rms_scale(v):
    return lax.rsqrt(jnp.mean(v * v, axis=-1, keepdims=True) + EPS)


def _fold_kernel(cc_ref, sc_ref, wf_ref, m_ref):
    scale = 1.0 / np.sqrt(float(SEQ * GROUP_DIM))
    for g in range(N_GROUPS):
        w = wf_ref[g]
        m1 = jnp.dot(cc_ref[...], w, preferred_element_type=F32, precision=lax.Precision.HIGHEST)
        m2 = jnp.dot(sc_ref[...], w, preferred_element_type=F32, precision=lax.Precision.HIGHEST)
        m_ref[g, :GROUP_DIM, :] = (m1 * scale).astype(BF16)
        m_ref[g, GROUP_DIM:, :] = (m2 * (-scale)).astype(BF16)


def _fold_fourier_weights(fourier_w):
    cc, sc = _dft_tables(GROUP_DIM)
    return pl.pallas_call(
        _fold_kernel,
        out_shape=jax.ShapeDtypeStruct((N_GROUPS, 2 * GROUP_DIM, GROUP_DIM), BF16),
        name="fold_fourier_w",
    )(jnp.asarray(cc), jnp.asarray(sc), fourier_w)


def _gelu(z):
    return 0.5 * z * (1.0 + lax.erf(z * np.float32(np.sqrt(0.5))))


def _in_proj_kernel(x_ref, g_ref, w_ref, gv_ref, ws_ref, bs_ref, a_ref, yg_ref):
    x = x_ref[...]
    xn = (x * _rms_scale(x) * g_ref[...]).astype(BF16)
    a_ref[...] = _dot(xn, w_ref[:, :FOURIER_WIDTH]).astype(BF16)
    n_chunks = x.shape[0] // CHUNK
    for h in range(N_HEADS):
        cu = FOURIER_WIDTH + h * GROUP_DIM
        cv = cu + GMLP_WIDTH
        u = _gelu(_dot(xn, w_ref[:, cu:cu + GROUP_DIM]))
        v = _gelu(_dot(xn, w_ref[:, cv:cv + GROUP_DIM]))
        vn = (v * _rms_scale(v) * gv_ref[h:h + 1, :]).astype(BF16)
        ws = ws_ref[h]
        for c in range(n_chunks):
            rows = slice(c * CHUNK, (c + 1) * CHUNK)
            s = _dot(ws, vn[rows, :]) + bs_ref[h]
            yg_ref[rows, h * GROUP_DIM:(h + 1) * GROUP_DIM] = (u[rows, :] * s).astype(BF16)


def _in_proj(x2, norm_g, w_in, gv, ws, bs):
    rows = x2.shape[0]
    const = lambda *shape: pl.BlockSpec(shape, lambda i: (0,) * len(shape),
                                        pipeline_mode=pl.Buffered(1))
    return pl.pallas_call(
        _in_proj_kernel,
        grid=(rows // TM_IN,),
        in_specs=[
            pl.BlockSpec((TM_IN, D_MODEL), lambda i: (i, 0)),
            const(1, D_MODEL),
            const(D_MODEL, IN_PROJ_WIDTH),
            const(N_HEADS, GROUP_DIM),
            const(N_HEADS, CHUNK, CHUNK),
            const(N_HEADS, CHUNK, GROUP_DIM),
        ],
        out_specs=[
            pl.BlockSpec((TM_IN, FOURIER_WIDTH), lambda i: (i, 0)),
            pl.BlockSpec((TM_IN, GMLP_WIDTH), lambda i: (i, 0)),
        ],
        out_shape=[
            jax.ShapeDtypeStruct((rows, FOURIER_WIDTH), BF16),
            jax.ShapeDtypeStruct((rows, GMLP_WIDTH), BF16),
        ],
        compiler_params=pltpu.CompilerParams(
            dimension_semantics=("arbitrary",), vmem_limit_bytes=VMEM_LIMIT_BYTES),
        name="in_proj_gmlp",
    )(x2, norm_g, w_in, gv, ws, bs)


def _fourier_kernel(cs_ref, ss_ref, a_ref, m_ref, yf_ref):
    a = a_ref[...]
    pc = _dot(cs_ref[...], a).astype(BF16)
    ps = _dot(ss_ref[...], a).astype(BF16)
    for g in range(N_GROUPS):
        cols = slice(g * GROUP_DIM, (g + 1) * GROUP_DIM)
        y = _dot(pc[:, cols], m_ref[g, :GROUP_DIM, :]) + _dot(ps[:, cols], m_ref[g, GROUP_DIM:, :])
        yf_ref[:, cols] = y.astype(BF16)


def _fourier(a, m, batch):
    cs, ss = _dft_tables(SEQ)
    cs = jnp.asarray(cs).astype(BF16)
    ss = jnp.asarray(ss).astype(BF16)
    kt = SEQ // TK_FOURIER
    return pl.pallas_call(
        _fourier_kernel,
        grid=(batch, kt),
        in_specs=[
            pl.BlockSpec((TK_FOURIER, SEQ), lambda b, k: (k, 0)),
            pl.BlockSpec((TK_FOURIER, SEQ), lambda b, k: (k, 0)),
            pl.BlockSpec((SEQ, FOURIER_WIDTH), lambda b, k: (b, 0)),
            pl.BlockSpec((N_GROUPS, 2 * GROUP_DIM, GROUP_DIM), lambda b, k: (0, 0, 0),
                         pipeline_mode=pl.Buffered(1)),
        ],
        out_specs=pl.BlockSpec((TK_FOURIER, FOURIER_WIDTH), lambda b, k: (b * kt + k, 0)),
        out_shape=jax.ShapeDtypeStruct((batch * SEQ, FOURIER_WIDTH), BF16),
        compiler_params=pltpu.CompilerParams(
            dimension_semantics=("arbitrary", "arbitrary"), vmem_limit_bytes=VMEM_LIMIT_BYTES),
        name="fourier_mix",
    )(cs, ss, a, m)


def _out_proj_kernel(yf_ref, yg_ref, x_ref, w_ref, g_ref, h_ref, xn_ref):
    h = x_ref[...] + _dot(yf_ref[...], w_ref[:FOURIER_WIDTH, :]) + _dot(yg_ref[...], w_ref[FOURIER_WIDTH:, :])
    h_ref[...] = h
    xn_ref[...] = (h * _rms_scale(h) * g_ref[...]).astype(BF16)


def _out_proj(yf, yg, x2, w_out, norm_g):
    rows = x2.shape[0]
    return pl.pallas_call(
        _out_proj_kernel,
        grid=(rows // TM_OUT,),
        in_specs=[
            pl.BlockSpec((TM_OUT, FOURIER_WIDTH), lambda i: (i, 0)),
            pl.BlockSpec((TM_OUT, GMLP_WIDTH), lambda i: (i, 0)),
            pl.BlockSpec((TM_OUT, D_MODEL), lambda i: (i, 0)),
            pl.BlockSpec((D_MODEL, D_MODEL), lambda i: (0, 0), pipeline_mode=pl.Buffered(1)),
            pl.BlockSpec((1, D_MODEL), lambda i: (0, 0), pipeline_mode=pl.Buffered(1)),
        ],
        out_specs=[
            pl.BlockSpec((TM_OUT, D_MODEL), lambda i: (i, 0)),
            pl.BlockSpec((TM_OUT, D_MODEL), lambda i: (i, 0)),
        ],
        out_shape=[
            jax.ShapeDtypeStruct((rows, D_MODEL), F32),
            jax.ShapeDtypeStruct((rows, D_MODEL), BF16),
        ],
        compiler_params=pltpu.CompilerParams(
            dimension_semantics=("arbitrary",), vmem_limit_bytes=VMEM_LIMIT_BYTES),
        name="out_proj",
    )(yf, yg, x2, w_out, norm_g)


def _mlp_kernel(xn_ref, h_ref, wu_ref, wd_ref, g_ref, o_ref):
    f = pl.program_id(1)
    r = jnp.maximum(_dot(xn_ref[...], wu_ref[...]), 0.0)
    d = _dot((r * r).astype(BF16), wd_ref[...])

    @pl.when(f == 0)
    def _():
        o_ref[...] = h_ref[...] + d

    @pl.when(f > 0)
    def _():
        o_ref[...] += d

    @pl.when(f == pl.num_programs(1) - 1)
    def _():
        h = o_ref[...]
        o_ref[...] = h * _rms_scale(h) * g_ref[...]


def _mlp(xn, h, w_up, w_down, norm_g):
    rows = xn.shape[0]
    return pl.pallas_call(
        _mlp_kernel,
        grid=(rows // TM_MLP, D_FF // TF_MLP),
        in_specs=[
            pl.BlockSpec((TM_MLP, D_MODEL), lambda i, f: (i, 0)),
            pl.BlockSpec((TM_MLP, D_MODEL), lambda i, f: (i, 0)),
            pl.BlockSpec((D_MODEL, TF_MLP), lambda i, f: (0, f)),
            pl.BlockSpec((TF_MLP, D_MODEL), lambda i, f: (f, 0)),
            pl.BlockSpec((1, D_MODEL), lambda i, f: (0, 0), pipeline_mode=pl.Buffered(1)),
        ],
        out_specs=pl.BlockSpec((TM_MLP, D_MODEL), lambda i, f: (i, 0)),
        out_shape=jax.ShapeDtypeStruct((rows, D_MODEL), F32),
        compiler_params=pltpu.CompilerParams(
            dimension_semantics=("arbitrary", "arbitrary"), vmem_limit_bytes=VMEM_LIMIT_BYTES),
        name="mlp",
    )(xn, h, w_up, w_down, norm_g)


def kernel(x, norm_mix_g, w_in, fourier_w, gmlp_v_g, gmlp_ws, gmlp_b, w_out,
           norm_mlp_g, w_up, w_down, norm_final_g):
    batch, seq, d = x.shape
    assert (seq, d) == (SEQ, D_MODEL)
    x2 = x.reshape(batch * seq, d)
    bs = jnp.broadcast_to(gmlp_b[:, :, None], (N_HEADS, CHUNK, GROUP_DIM))

    m = _fold_fourier_weights(fourier_w)
    a, yg = _in_proj(x2, norm_mix_g.reshape(1, d), w_in.astype(BF16), gmlp_v_g,
                     gmlp_ws.astype(BF16), bs)
    yf = _fourier(a, m, batch)
    h, xn = _out_proj(yf, yg, x2, w_out.astype(BF16), norm_mlp_g.reshape(1, d))
    y = _mlp(xn, h, w_up.astype(BF16), w_down.astype(BF16), norm_final_g.reshape(1, d))
    return y.reshape(batch, seq, d)
```

```python
import functools

import numpy as np
import jax
import jax.numpy as jnp
from jax import lax
from jax.experimental import pallas as pl
from jax.experimental.pallas import tpu as pltpu

D_MODEL = 2048
SEQ = 2048
GROUP_DIM = 128
FOURIER_WIDTH = 1024
GMLP_WIDTH = 1024
N_GROUPS = FOURIER_WIDTH // GROUP_DIM
N_HEADS = GMLP_WIDTH // GROUP_DIM
IN_PROJ_WIDTH = FOURIER_WIDTH + 2 * GMLP_WIDTH
CHUNK = 128
D_FF = 4 * D_MODEL
EPS = 1e-6

F32 = jnp.float32
BF16 = jnp.bfloat16

VMEM_LIMIT_BYTES = 56 * 1024 * 1024

TM_IN = 512
HEADS_PER_DOT = 4
TK_FOURIER = 512
TM_OUT = 512
TM_MLP = 1024
TF_MLP = 512
SUB_MLP = 512


def _dft_tables(n):
    k = np.arange(n, dtype=np.int64)
    ang = 2.0 * np.pi * ((k[:, None] * k[None, :]) % n).astype(np.float64) / n
    return np.cos(ang).astype(np.float32), np.sin(ang).astype(np.float32)


def _dot(a, b):
    return jnp.dot(a, b, preferred_element_type=F32)


def _rms_scale(v):
    return lax.rsqrt(jnp.mean(v * v, axis=-1, keepdims=True) + EPS)


def _fold_kernel(cc_ref, sc_ref, wf_ref, m_ref):
    scale = 1.0 / np.sqrt(float(SEQ * GROUP_DIM))
    for g in range(N_GROUPS):
        w = wf_ref[g]
        m1 = jnp.dot(cc_ref[...], w, preferred_element_type=F32, precision=lax.Precision.HIGHEST)
        m2 = jnp.dot(sc_ref[...], w, preferred_element_type=F32, precision=lax.Precision.HIGHEST)
        m_ref[g, :GROUP_DIM, :] = (m1 * scale).astype(BF16)
        m_ref[g, GROUP_DIM:, :] = (m2 * (-scale)).astype(BF16)


def _fold_fourier_weights(fourier_w):
    cc, sc = _dft_tables(GROUP_DIM)
    return pl.pallas_call(
        _fold_kernel,
        out_shape=jax.ShapeDtypeStruct((N_GROUPS, 2 * GROUP_DIM, GROUP_DIM), BF16),
        name="fold_fourier_w",
    )(jnp.asarray(cc), jnp.asarray(sc), fourier_w)


def _gelu(z):
    return 0.5 * z * (1.0 + lax.erf(z * np.float32(np.sqrt(0.5))))


def _in_proj_kernel(x_ref, g_ref, w_ref, gv_ref, ws_ref, bs_ref, a_ref, yg_ref):
    x = x_ref[...]
    xn = (x * _rms_scale(x) * g_ref[...]).astype(BF16)
    a_ref[...] = _dot(xn, w_ref[:, :FOURIER_WIDTH]).astype(BF16)
    n_chunks = x.shape[0] // CHUNK
    width = HEADS_PER_DOT * GROUP_DIM
    for hb in range(N_HEADS // HEADS_PER_DOT):
        cu = FOURIER_WIDTH + hb * width
        cv = cu + GMLP_WIDTH
        u = _gelu(_dot(xn, w_ref[:, cu:cu + width]))
        v = _gelu(_dot(xn, w_ref[:, cv:cv + width]))
        for j in range(HEADS_PER_DOT):
            h = hb * HEADS_PER_DOT + j
            cols = slice(j * GROUP_DIM, (j + 1) * GROUP_DIM)
            vh = v[:, cols]
            vn = (vh * _rms_scale(vh) * gv_ref[h:h + 1, :]).astype(BF16)
            vcat = jnp.concatenate(
                [vn[c * CHUNK:(c + 1) * CHUNK, :] for c in range(n_chunks)], axis=1)
            s = _dot(ws_ref[h], vcat)
            for c in range(n_chunks):
                rows = slice(c * CHUNK, (c + 1) * CHUNK)
                sc = s[:, c * GROUP_DIM:(c + 1) * GROUP_DIM] + bs_ref[h]
                yg_ref[rows, h * GROUP_DIM:(h + 1) * GROUP_DIM] = (u[rows, cols] * sc).astype(BF16)


def _in_proj(x2, norm_g, w_in, gv, ws, bs):
    rows = x2.shape[0]
    const = lambda *shape: pl.BlockSpec(shape, lambda i: (0,) * len(shape),
                                        pipeline_mode=pl.Buffered(1))
    return pl.pallas_call(
        _in_proj_kernel,
        grid=(rows // TM_IN,),
        in_specs=[
            pl.BlockSpec((TM_IN, D_MODEL), lambda i: (i, 0)),
            const(1, D_MODEL),
            const(D_MODEL, IN_PROJ_WIDTH),
            const(N_HEADS, GROUP_DIM),
            const(N_HEADS, CHUNK, CHUNK),
            const(N_HEADS, CHUNK, GROUP_DIM),
        ],
        out_specs=[
            pl.BlockSpec((TM_IN, FOURIER_WIDTH), lambda i: (i, 0)),
            pl.BlockSpec((TM_IN, GMLP_WIDTH), lambda i: (i, 0)),
        ],
        out_shape=[
            jax.ShapeDtypeStruct((rows, FOURIER_WIDTH), BF16),
            jax.ShapeDtypeStruct((rows, GMLP_WIDTH), BF16),
        ],
        compiler_params=pltpu.CompilerParams(
            dimension_semantics=("arbitrary",), vmem_limit_bytes=VMEM_LIMIT_BYTES),
        name="in_proj_gmlp",
    )(x2, norm_g, w_in, gv, ws, bs)


def _fourier_kernel(cs_ref, ss_ref, a_ref, m_ref, yf_ref):
    a = a_ref[...]
    pc = _dot(cs_ref[...], a).astype(BF16)
    ps = _dot(ss_ref[...], a).astype(BF16)
    for g in range(N_GROUPS):
        cols = slice(g * GROUP_DIM, (g + 1) * GROUP_DIM)
        y = _dot(pc[:, cols], m_ref[g, :GROUP_DIM, :]) + _dot(ps[:, cols], m_ref[g, GROUP_DIM:, :])
        yf_ref[:, cols] = y.astype(BF16)


def _fourier(a, m, batch):
    cs, ss = _dft_tables(SEQ)
    cs = jnp.asarray(cs).astype(BF16)
    ss = jnp.asarray(ss).astype(BF16)
    kt = SEQ // TK_FOURIER
    return pl.pallas_call(
        _fourier_kernel,
        grid=(batch, kt),
        in_specs=[
            pl.BlockSpec((TK_FOURIER, SEQ), lambda b, k: (k, 0)),
            pl.BlockSpec((TK_FOURIER, SEQ), lambda b, k: (k, 0)),
            pl.BlockSpec((SEQ, FOURIER_WIDTH), lambda b, k: (b, 0)),
            pl.BlockSpec((N_GROUPS, 2 * GROUP_DIM, GROUP_DIM), lambda b, k: (0, 0, 0),
                         pipeline_mode=pl.Buffered(1)),
        ],
        out_specs=pl.BlockSpec((TK_FOURIER, FOURIER_WIDTH), lambda b, k: (b * kt + k, 0)),
        out_shape=jax.ShapeDtypeStruct((batch * SEQ, FOURIER_WIDTH), BF16),
        compiler_params=pltpu.CompilerParams(
            dimension_semantics=("arbitrary", "arbitrary"), vmem_limit_bytes=VMEM_LIMIT_BYTES),
        name="fourier_mix",
    )(cs, ss, a, m)


def _out_proj_kernel(yf_ref, yg_ref, x_ref, w_ref, g_ref, h_ref, xn_ref):
    h = x_ref[...] + _dot(yf_ref[...], w_ref[:FOURIER_WIDTH, :]) + _dot(yg_ref[...], w_ref[FOURIER_WIDTH:, :])
    h_ref[...] = h
    xn_ref[...] = (h * _rms_scale(h) * g_ref[...]).astype(BF16)


def _out_proj(yf, yg, x2, w_out, norm_g):
    rows = x2.shape[0]
    return pl.pallas_call(
        _out_proj_kernel,
        grid=(rows // TM_OUT,),
        in_specs=[
            pl.BlockSpec((TM_OUT, FOURIER_WIDTH), lambda i: (i, 0)),
            pl.BlockSpec((TM_OUT, GMLP_WIDTH), lambda i: (i, 0)),
            pl.BlockSpec((TM_OUT, D_MODEL), lambda i: (i, 0)),
            pl.BlockSpec((D_MODEL, D_MODEL), lambda i: (0, 0), pipeline_mode=pl.Buffered(1)),
            pl.BlockSpec((1, D_MODEL), lambda i: (0, 0), pipeline_mode=pl.Buffered(1)),
        ],
        out_specs=[
            pl.BlockSpec((TM_OUT, D_MODEL), lambda i: (i, 0)),
            pl.BlockSpec((TM_OUT, D_MODEL), lambda i: (i, 0)),
        ],
        out_shape=[
            jax.ShapeDtypeStruct((rows, D_MODEL), F32),
            jax.ShapeDtypeStruct((rows, D_MODEL), BF16),
        ],
        compiler_params=pltpu.CompilerParams(
            dimension_semantics=("arbitrary",), vmem_limit_bytes=VMEM_LIMIT_BYTES),
        name="out_proj",
    )(yf, yg, x2, w_out, norm_g)


def _mlp_kernel(xn_ref, h_ref, wu_ref, wd_ref, g_ref, o_ref):
    f = pl.program_id(1)

    @pl.when(f == 0)
    def _():
        o_ref[...] = h_ref[...]

    for rb in range(TM_MLP // SUB_MLP):
        rows = slice(rb * SUB_MLP, (rb + 1) * SUB_MLP)
        r = jnp.maximum(_dot(xn_ref[rows, :], wu_ref[...]), 0.0)
        o_ref[rows, :] += _dot((r * r).astype(BF16), wd_ref[...])

    @pl.when(f == pl.num_programs(1) - 1)
    def _():
        h = o_ref[...]
        o_ref[...] = h * _rms_scale(h) * g_ref[...]


def _mlp(xn, h, w_up, w_down, norm_g):
    rows = xn.shape[0]
    return pl.pallas_call(
        _mlp_kernel,
        grid=(rows // TM_MLP, D_FF // TF_MLP),
        in_specs=[
            pl.BlockSpec((TM_MLP, D_MODEL), lambda i, f: (i, 0)),
            pl.BlockSpec((TM_MLP, D_MODEL), lambda i, f: (i, 0)),
            pl.BlockSpec((D_MODEL, TF_MLP), lambda i, f: (0, f)),
            pl.BlockSpec((TF_MLP, D_MODEL), lambda i, f: (f, 0)),
            pl.BlockSpec((1, D_MODEL), lambda i, f: (0, 0), pipeline_mode=pl.Buffered(1)),
        ],
        out_specs=pl.BlockSpec((TM_MLP, D_MODEL), lambda i, f: (i, 0)),
        out_shape=jax.ShapeDtypeStruct((rows, D_MODEL), F32),
        compiler_params=pltpu.CompilerParams(
            dimension_semantics=("arbitrary", "arbitrary"), vmem_limit_bytes=VMEM_LIMIT_BYTES),
        name="mlp",
    )(xn, h, w_up, w_down, norm_g)


def kernel(x, norm_mix_g, w_in, fourier_w, gmlp_v_g, gmlp_ws, gmlp_b, w_out,
           norm_mlp_g, w_up, w_down, norm_final_g):
    batch, seq, d = x.shape
    assert (seq, d) == (SEQ, D_MODEL)
    x2 = x.reshape(batch * seq, d)
    bs = jnp.broadcast_to(gmlp_b[:, :, None], (N_HEADS, CHUNK, GROUP_DIM))

    m = _fold_fourier_weights(fourier_w)
    a, yg = _in_proj(x2, norm_mix_g.reshape(1, d), w_in.astype(BF16), gmlp_v_g,
                     gmlp_ws.astype(BF16), bs)
    yf = _fourier(a, m, batch)
    h, xn = _out_proj(yf, yg, x2, w_out.astype(BF16), norm_mlp_g.reshape(1, d))
    y = _mlp(xn, h, w_up.astype(BF16), w_down.astype(BF16), norm_final_g.reshape(1, d))
    return y.reshape(batch, seq, d)
```

```python
import functools

import numpy as np
import jax
import jax.numpy as jnp
from jax import lax
from jax.experimental import pallas as pl
from jax.experimental.pallas import tpu as pltpu

D_MODEL = 2048
SEQ = 2048
GROUP_DIM = 128
FOURIER_WIDTH = 1024
GMLP_WIDTH = 1024
N_GROUPS = FOURIER_WIDTH // GROUP_DIM
N_HEADS = GMLP_WIDTH // GROUP_DIM
IN_PROJ_WIDTH = FOURIER_WIDTH + 2 * GMLP_WIDTH
CHUNK = 128
D_FF = 4 * D_MODEL
EPS = 1e-6

F32 = jnp.float32
BF16 = jnp.bfloat16

VMEM_LIMIT_BYTES = 56 * 1024 * 1024

TM_IN = 512
HEADS_PER_DOT = 4
TK_FOURIER = 512
TM_OUT = 512
TM_MLP = 1024
TF_MLP = 512
SUB_MLP = 512


def _dft_tables(n):
    k = np.arange(n, dtype=np.int64)
    ang = 2.0 * np.pi * ((k[:, None] * k[None, :]) % n).astype(np.float64) / n
    return np.cos(ang).astype(np.float32), np.sin(ang).astype(np.float32)


def _dot(a, b):
    return jnp.dot(a, b, preferred_element_type=F32)


def _rms_scale(v):
    return lax.rsqrt(jnp.mean(v * v, axis=-1, keepdims=True) + EPS)


def _fold_kernel(cc_ref, sc_ref, wf_ref, m_ref):
    scale = 1.0 / np.sqrt(float(SEQ * GROUP_DIM))
    for g in range(N_GROUPS):
        w = wf_ref[g]
        m1 = jnp.dot(cc_ref[...], w, preferred_element_type=F32, precision=lax.Precision.HIGHEST)
        m2 = jnp.dot(sc_ref[...], w, preferred_element_type=F32, precision=lax.Precision.HIGHEST)
        m_ref[g, :GROUP_DIM, :] = (m1 * scale).astype(BF16)
        m_ref[g, GROUP_DIM:, :] = (m2 * (-scale)).astype(BF16)


def _fold_fourier_weights(fourier_w):
    cc, sc = _dft_tables(GROUP_DIM)
    return pl.pallas_call(
        _fold_kernel,
        out_shape=jax.ShapeDtypeStruct((N_GROUPS, 2 * GROUP_DIM, GROUP_DIM), BF16),
        name="fold_fourier_w",
    )(jnp.asarray(cc), jnp.asarray(sc), fourier_w)


def _gelu(z):
    return 0.5 * z * (1.0 + lax.erf(z * np.float32(np.sqrt(0.5))))


def _in_proj_kernel(x_ref, g_ref, w_ref, gv_ref, ws_ref, bs_ref, wu_ref, wd_ref, wo_ref,
                    a_ref, yg_ref, wu_o_ref, wd_o_ref, wo_o_ref):
    wu_o_ref[0] = wu_ref[...].astype(BF16)
    wd_o_ref[...] = wd_ref[...].astype(BF16)
    wo_o_ref[...] = wo_ref[...].astype(BF16)

    x = x_ref[...]
    xn = (x * _rms_scale(x) * g_ref[...]).astype(BF16)
    a_ref[...] = _dot(xn, w_ref[:, :FOURIER_WIDTH]).astype(BF16)
    n_chunks = x.shape[0] // CHUNK
    width = HEADS_PER_DOT * GROUP_DIM
    for hb in range(N_HEADS // HEADS_PER_DOT):
        cu = FOURIER_WIDTH + hb * width
        cv = cu + GMLP_WIDTH
        u = _gelu(_dot(xn, w_ref[:, cu:cu + width]))
        v = _gelu(_dot(xn, w_ref[:, cv:cv + width]))
        for j in range(HEADS_PER_DOT):
            h = hb * HEADS_PER_DOT + j
            cols = slice(j * GROUP_DIM, (j + 1) * GROUP_DIM)
            vh = v[:, cols]
            vn = (vh * _rms_scale(vh) * gv_ref[h:h + 1, :]).astype(BF16)
            vcat = jnp.concatenate(
                [vn[c * CHUNK:(c + 1) * CHUNK, :] for c in range(n_chunks)], axis=1)
            s = _dot(ws_ref[h], vcat)
            for c in range(n_chunks):
                rows = slice(c * CHUNK, (c + 1) * CHUNK)
                sc = s[:, c * GROUP_DIM:(c + 1) * GROUP_DIM] + bs_ref[h]
                yg_ref[rows, h * GROUP_DIM:(h + 1) * GROUP_DIM] = (u[rows, cols] * sc).astype(BF16)


def _in_proj(x2, norm_g, w_in, gv, ws, bs, w_up, w_down, w_out):
    rows = x2.shape[0]
    steps = rows // TM_IN
    ff_cols = D_FF // steps
    out_rows = D_MODEL // steps
    per_tile = TF_MLP // ff_cols
    assert ff_cols * steps == D_FF and out_rows * steps == D_MODEL and per_tile * ff_cols == TF_MLP
    const = lambda *shape: pl.BlockSpec(shape, lambda i: (0,) * len(shape),
                                        pipeline_mode=pl.Buffered(1))
    return pl.pallas_call(
        _in_proj_kernel,
        grid=(steps,),
        in_specs=[
            pl.BlockSpec((TM_IN, D_MODEL), lambda i: (i, 0)),
            const(1, D_MODEL),
            const(D_MODEL, IN_PROJ_WIDTH),
            const(N_HEADS, GROUP_DIM),
            const(N_HEADS, CHUNK, CHUNK),
            const(N_HEADS, CHUNK, GROUP_DIM),
            pl.BlockSpec((D_MODEL, ff_cols), lambda i: (0, i)),
            pl.BlockSpec((ff_cols, D_MODEL), lambda i: (i, 0)),
            pl.BlockSpec((out_rows, D_MODEL), lambda i: (i, 0)),
        ],
        out_specs=[
            pl.BlockSpec((TM_IN, FOURIER_WIDTH), lambda i: (i, 0)),
            pl.BlockSpec((TM_IN, GMLP_WIDTH), lambda i: (i, 0)),
            pl.BlockSpec((1, D_MODEL, ff_cols), lambda i: (i // per_tile, 0, i % per_tile)),
            pl.BlockSpec((ff_cols, D_MODEL), lambda i: (i, 0)),
            pl.BlockSpec((out_rows, D_MODEL), lambda i: (i, 0)),
        ],
        out_shape=[
            jax.ShapeDtypeStruct((rows, FOURIER_WIDTH), BF16),
            jax.ShapeDtypeStruct((rows, GMLP_WIDTH), BF16),
            jax.ShapeDtypeStruct((D_FF // TF_MLP, D_MODEL, TF_MLP), BF16),
            jax.ShapeDtypeStruct((D_FF, D_MODEL), BF16),
            jax.ShapeDtypeStruct((D_MODEL, D_MODEL), BF16),
        ],
        compiler_params=pltpu.CompilerParams(
            dimension_semantics=("arbitrary",), vmem_limit_bytes=VMEM_LIMIT_BYTES),
        name="in_proj_gmlp",
    )(x2, norm_g, w_in, gv, ws, bs, w_up, w_down, w_out)


def _fourier_kernel(cs_ref, ss_ref, a_ref, m_ref, yf_ref):
    a = a_ref[...]
    pc = _dot(cs_ref[...], a).astype(BF16)
    ps = _dot(ss_ref[...], a).astype(BF16)
    for g in range(N_GROUPS):
        cols = slice(g * GROUP_DIM, (g + 1) * GROUP_DIM)
        y = _dot(pc[:, cols], m_ref[g, :GROUP_DIM, :]) + _dot(ps[:, cols], m_ref[g, GROUP_DIM:, :])
        yf_ref[:, cols] = y.astype(BF16)


def _fourier(a, m, batch):
    cs, ss = _dft_tables(SEQ)
    cs = jnp.asarray(cs).astype(BF16)
    ss = jnp.asarray(ss).astype(BF16)
    kt = SEQ // TK_FOURIER
    return pl.pallas_call(
        _fourier_kernel,
        grid=(batch, kt),
        in_specs=[
            pl.BlockSpec((TK_FOURIER, SEQ), lambda b, k: (k, 0)),
            pl.BlockSpec((TK_FOURIER, SEQ), lambda b, k: (k, 0)),
            pl.BlockSpec((SEQ, FOURIER_WIDTH), lambda b, k: (b, 0)),
            pl.BlockSpec((N_GROUPS, 2 * GROUP_DIM, GROUP_DIM), lambda b, k: (0, 0, 0),
                         pipeline_mode=pl.Buffered(1)),
        ],
        out_specs=pl.BlockSpec((TK_FOURIER, FOURIER_WIDTH), lambda b, k: (b * kt + k, 0)),
        out_shape=jax.ShapeDtypeStruct((batch * SEQ, FOURIER_WIDTH), BF16),
        compiler_params=pltpu.CompilerParams(
            dimension_semantics=("arbitrary", "arbitrary"), vmem_limit_bytes=VMEM_LIMIT_BYTES),
        name="fourier_mix",
    )(cs, ss, a, m)


def _out_proj_kernel(yf_ref, yg_ref, x_ref, w_ref, g_ref, h_ref, xn_ref):
    h = x_ref[...] + _dot(yf_ref[...], w_ref[:FOURIER_WIDTH, :]) + _dot(yg_ref[...], w_ref[FOURIER_WIDTH:, :])
    h_ref[...] = h
    xn_ref[...] = (h * _rms_scale(h) * g_ref[...]).astype(BF16)


def _out_proj(yf, yg, x2, w_out, norm_g):
    rows = x2.shape[0]
    return pl.pallas_call(
        _out_proj_kernel,
        grid=(rows // TM_OUT,),
        in_specs=[
            pl.BlockSpec((TM_OUT, FOURIER_WIDTH), lambda i: (i, 0)),
            pl.BlockSpec((TM_OUT, GMLP_WIDTH), lambda i: (i, 0)),
            pl.BlockSpec((TM_OUT, D_MODEL), lambda i: (i, 0)),
            pl.BlockSpec((D_MODEL, D_MODEL), lambda i: (0, 0), pipeline_mode=pl.Buffered(1)),
            pl.BlockSpec((1, D_MODEL), lambda i: (0, 0), pipeline_mode=pl.Buffered(1)),
        ],
        out_specs=[
            pl.BlockSpec((TM_OUT, D_MODEL), lambda i: (i, 0)),
            pl.BlockSpec((TM_OUT, D_MODEL), lambda i: (i, 0)),
        ],
        out_shape=[
            jax.ShapeDtypeStruct((rows, D_MODEL), F32),
            jax.ShapeDtypeStruct((rows, D_MODEL), BF16),
        ],
        compiler_params=pltpu.CompilerParams(
            dimension_semantics=("arbitrary",), vmem_limit_bytes=VMEM_LIMIT_BYTES),
        name="out_proj",
    )(yf, yg, x2, w_out, norm_g)


def _mlp_kernel(xn_ref, h_ref, wu_ref, wd_ref, g_ref, o_ref):
    f = pl.program_id(1)

    @pl.when(f == 0)
    def _():
        o_ref[...] = h_ref[...]

    for rb in range(TM_MLP // SUB_MLP):
        rows = slice(rb * SUB_MLP, (rb + 1) * SUB_MLP)
        r = jnp.maximum(_dot(xn_ref[rows, :], wu_ref[0]), 0.0)
        o_ref[rows, :] += _dot((r * r).astype(BF16), wd_ref[...])

    @pl.when(f == pl.num_programs(1) - 1)
    def _():
        h = o_ref[...]
        o_ref[...] = h * _rms_scale(h) * g_ref[...]


def _mlp(xn, h, w_up, w_down, norm_g):
    rows = xn.shape[0]
    return pl.pallas_call(
        _mlp_kernel,
        grid=(rows // TM_MLP, D_FF // TF_MLP),
        in_specs=[
            pl.BlockSpec((TM_MLP, D_MODEL), lambda i, f: (i, 0)),
            pl.BlockSpec((TM_MLP, D_MODEL), lambda i, f: (i, 0)),
            pl.BlockSpec((1, D_MODEL, TF_MLP), lambda i, f: (f, 0, 0)),
            pl.BlockSpec((TF_MLP, D_MODEL), lambda i, f: (f, 0)),
            pl.BlockSpec((1, D_MODEL), lambda i, f: (0, 0), pipeline_mode=pl.Buffered(1)),
        ],
        out_specs=pl.BlockSpec((TM_MLP, D_MODEL), lambda i, f: (i, 0)),
        out_shape=jax.ShapeDtypeStruct((rows, D_MODEL), F32),
        compiler_params=pltpu.CompilerParams(
            dimension_semantics=("arbitrary", "arbitrary"), vmem_limit_bytes=VMEM_LIMIT_BYTES),
        name="mlp",
    )(xn, h, w_up, w_down, norm_g)


def kernel(x, norm_mix_g, w_in, fourier_w, gmlp_v_g, gmlp_ws, gmlp_b, w_out,
           norm_mlp_g, w_up, w_down, norm_final_g):
    batch, seq, d = x.shape
    assert (seq, d) == (SEQ, D_MODEL)
    x2 = x.reshape(batch * seq, d)
    bs = jnp.broadcast_to(gmlp_b[:, :, None], (N_HEADS, CHUNK, GROUP_DIM))

    m = _fold_fourier_weights(fourier_w)
    a, yg, w_up_b, w_down_b, w_out_b = _in_proj(
        x2, norm_mix_g.reshape(1, d), w_in.astype(BF16), gmlp_v_g, gmlp_ws.astype(BF16), bs,
        w_up, w_down, w_out)
    yf = _fourier(a, m, batch)
    h, xn = _out_proj(yf, yg, x2, w_out_b, norm_mlp_g.reshape(1, d))
    y = _mlp(xn, h, w_up_b, w_down_b, norm_final_g.reshape(1, d))
    return y.reshape(batch, seq, d)
```

```python
import functools

import numpy as np
import jax
import jax.numpy as jnp
from jax import lax
from jax.experimental import pallas as pl
from jax.experimental.pallas import tpu as pltpu

D_MODEL = 2048
SEQ = 2048
GROUP_DIM = 128
FOURIER_WIDTH = 1024
GMLP_WIDTH = 1024
N_GROUPS = FOURIER_WIDTH // GROUP_DIM
N_HEADS = GMLP_WIDTH // GROUP_DIM
IN_PROJ_WIDTH = FOURIER_WIDTH + 2 * GMLP_WIDTH
CHUNK = 128
D_FF = 4 * D_MODEL
EPS = 1e-6

F32 = jnp.float32
BF16 = jnp.bfloat16

VMEM_LIMIT_BYTES = 56 * 1024 * 1024

TM_IN = 512
HEADS_PER_DOT = 4
RADIX = 4
TC_FOURIER = 512
TM_OUT = 512
TM_MLP = 1024
TF_MLP = 512
SUB_MLP = 512


def _dft_tables(n):
    k = np.arange(n, dtype=np.int64)
    ang = 2.0 * np.pi * ((k[:, None] * k[None, :]) % n).astype(np.float64) / n
    return np.cos(ang).astype(np.float32), np.sin(ang).astype(np.float32)


def _dot(a, b):
    return jnp.dot(a, b, preferred_element_type=F32)


def _rms_scale(v):
    return lax.rsqrt(jnp.mean(v * v, axis=-1, keepdims=True) + EPS)


def _fold_kernel(cc_ref, sc_ref, wf_ref, m_ref):
    scale = 1.0 / np.sqrt(float(SEQ * GROUP_DIM))
    for g in range(N_GROUPS):
        w = wf_ref[g]
        m1 = jnp.dot(cc_ref[...], w, preferred_element_type=F32, precision=lax.Precision.HIGHEST)
        m2 = jnp.dot(sc_ref[...], w, preferred_element_type=F32, precision=lax.Precision.HIGHEST)
        m_ref[g, :GROUP_DIM, :] = (m1 * scale).astype(BF16)
        m_ref[g, GROUP_DIM:, :] = (m2 * (-scale)).astype(BF16)


def _fold_fourier_weights(fourier_w):
    cc, sc = _dft_tables(GROUP_DIM)
    return pl.pallas_call(
        _fold_kernel,
        out_shape=jax.ShapeDtypeStruct((N_GROUPS, 2 * GROUP_DIM, GROUP_DIM), BF16),
        name="fold_fourier_w",
    )(jnp.asarray(cc), jnp.asarray(sc), fourier_w)


def _gelu(z):
    return 0.5 * z * (1.0 + lax.erf(z * np.float32(np.sqrt(0.5))))


def _in_proj_kernel(x_ref, g_ref, w_ref, gv_ref, ws_ref, bs_ref, wu_ref, wd_ref, wo_ref,
                    a_ref, yg_ref, wu_o_ref, wd_o_ref, wo_o_ref, pf_ref):
    wu_o_ref[0] = wu_ref[...].astype(BF16)
    wd_o_ref[...] = wd_ref[...].astype(BF16)
    wo_o_ref[...] = wo_ref[...].astype(BF16)

    x = x_ref[...]
    xn = (x * _rms_scale(x) * g_ref[...]).astype(BF16)
    pf = _dot(xn, w_ref[:, :FOURIER_WIDTH])
    for g in range(N_GROUPS):
        pf_ref[g] = pf[:, g * GROUP_DIM:(g + 1) * GROUP_DIM]
    for r in range(RADIX):
        for g in range(N_GROUPS):
            a_ref[0, r, :, g * GROUP_DIM:(g + 1) * GROUP_DIM] = (
                pf_ref[g, pl.ds(r, x.shape[0] // RADIX, stride=RADIX), :].astype(BF16))
    n_chunks = x.shape[0] // CHUNK
    width = HEADS_PER_DOT * GROUP_DIM
    for hb in range(N_HEADS // HEADS_PER_DOT):
        cu = FOURIER_WIDTH + hb * width
        cv = cu + GMLP_WIDTH
        u = _gelu(_dot(xn, w_ref[:, cu:cu + width]))
        v = _gelu(_dot(xn, w_ref[:, cv:cv + width]))
        for j in range(HEADS_PER_DOT):
            h = hb * HEADS_PER_DOT + j
            cols = slice(j * GROUP_DIM, (j + 1) * GROUP_DIM)
            vh = v[:, cols]
            vn = (vh * _rms_scale(vh) * gv_ref[h:h + 1, :]).astype(BF16)
            vcat = jnp.concatenate(
                [vn[c * CHUNK:(c + 1) * CHUNK, :] for c in range(n_chunks)], axis=1)
            s = _dot(ws_ref[h], vcat)
            for c in range(n_chunks):
                rows = slice(c * CHUNK, (c + 1) * CHUNK)
                sc = s[:, c * GROUP_DIM:(c + 1) * GROUP_DIM] + bs_ref[h]
                yg_ref[rows, h * GROUP_DIM:(h + 1) * GROUP_DIM] = (u[rows, cols] * sc).astype(BF16)


def _in_proj(x2, norm_g, w_in, gv, ws, bs, w_up, w_down, w_out):
    rows = x2.shape[0]
    steps = rows // TM_IN
    ff_cols = D_FF // steps
    out_rows = D_MODEL // steps
    per_tile = TF_MLP // ff_cols
    assert ff_cols * steps == D_FF and out_rows * steps == D_MODEL and per_tile * ff_cols == TF_MLP
    tiles_per_seq = SEQ // TM_IN
    res_rows = TM_IN // RADIX
    const = lambda *shape: pl.BlockSpec(shape, lambda i: (0,) * len(shape),
                                        pipeline_mode=pl.Buffered(1))
    return pl.pallas_call(
        _in_proj_kernel,
        grid=(steps,),
        in_specs=[
            pl.BlockSpec((TM_IN, D_MODEL), lambda i: (i, 0)),
            const(1, D_MODEL),
            const(D_MODEL, IN_PROJ_WIDTH),
            const(N_HEADS, GROUP_DIM),
            const(N_HEADS, CHUNK, CHUNK),
            const(N_HEADS, CHUNK, GROUP_DIM),
            pl.BlockSpec((D_MODEL, ff_cols), lambda i: (0, i)),
            pl.BlockSpec((ff_cols, D_MODEL), lambda i: (i, 0)),
            pl.BlockSpec((out_rows, D_MODEL), lambda i: (i, 0)),
        ],
        out_specs=[
            pl.BlockSpec((1, RADIX, res_rows, FOURIER_WIDTH),
                         lambda i: (i // tiles_per_seq, 0, i % tiles_per_seq, 0)),
            pl.BlockSpec((TM_IN, GMLP_WIDTH), lambda i: (i, 0)),
            pl.BlockSpec((1, D_MODEL, ff_cols), lambda i: (i // per_tile, 0, i % per_tile)),
            pl.BlockSpec((ff_cols, D_MODEL), lambda i: (i, 0)),
            pl.BlockSpec((out_rows, D_MODEL), lambda i: (i, 0)),
        ],
        out_shape=[
            jax.ShapeDtypeStruct((rows // SEQ, RADIX, SEQ // RADIX, FOURIER_WIDTH), BF16),
            jax.ShapeDtypeStruct((rows, GMLP_WIDTH), BF16),
            jax.ShapeDtypeStruct((D_FF // TF_MLP, D_MODEL, TF_MLP), BF16),
            jax.ShapeDtypeStruct((D_FF, D_MODEL), BF16),
            jax.ShapeDtypeStruct((D_MODEL, D_MODEL), BF16),
        ],
        scratch_shapes=[pltpu.VMEM((N_GROUPS, TM_IN, GROUP_DIM), F32)],
        compiler_params=pltpu.CompilerParams(
            dimension_semantics=("arbitrary",), vmem_limit_bytes=VMEM_LIMIT_BYTES),
        name="in_proj_gmlp",
    )(x2, norm_g, w_in, gv, ws, bs, w_up, w_down, w_out)


def _radix_tables():
    q = SEQ // RADIX
    kp = np.arange(q, dtype=np.int64)[:, None]
    m = np.arange(q, dtype=np.int64)[None, :]
    out = np.empty((RADIX, 2, q, q), np.float32)
    for r in range(RADIX):
        ang = 2.0 * np.pi * ((kp * (RADIX * m + r)) % SEQ).astype(np.float64) / SEQ
        out[r, 0] = np.cos(ang)
        out[r, 1] = np.sin(ang)
    return out


def _fourier_kernel(t_ref, a_ref, m_ref, yf_ref):
    q = SEQ // RADIX
    pc = [_dot(t_ref[r, 0], a_ref[0, r]) for r in range(RADIX)]
    ps = [_dot(t_ref[r, 1], a_ref[0, r]) for r in range(RADIX)]
    ac, bc, cc, dc = pc[0] + pc[2], pc[0] - pc[2], pc[1] + pc[3], pc[1] - pc[3]
    as_, bs, cs, ds = ps[0] + ps[2], ps[0] - ps[2], ps[1] + ps[3], ps[1] - ps[3]
    out_c = (ac + cc, bc - ds, ac - cc, bc + ds)
    out_s = (as_ + cs, bs + dc, as_ - cs, bs - dc)
    for j in range(RADIX):
        c = out_c[j].astype(BF16)
        s = out_s[j].astype(BF16)
        for g in range(m_ref.shape[0]):
            cols = slice(g * GROUP_DIM, (g + 1) * GROUP_DIM)
            y = _dot(jnp.concatenate([c[:, cols], s[:, cols]], axis=1), m_ref[g])
            yf_ref[j * q:(j + 1) * q, cols] = y.astype(BF16)


def _fourier(a, m, batch):
    assert RADIX == 4
    q = SEQ // RADIX
    tables = jnp.asarray(_radix_tables()).astype(BF16)
    groups = TC_FOURIER // GROUP_DIM
    return pl.pallas_call(
        _fourier_kernel,
        grid=(batch, FOURIER_WIDTH // TC_FOURIER),
        in_specs=[
            pl.BlockSpec((RADIX, 2, q, q), lambda b, c: (0, 0, 0, 0), pipeline_mode=pl.Buffered(1)),
            pl.BlockSpec((1, RADIX, q, TC_FOURIER), lambda b, c: (b, 0, 0, c)),
            pl.BlockSpec((groups, 2 * GROUP_DIM, GROUP_DIM), lambda b, c: (c, 0, 0)),
        ],
        out_specs=pl.BlockSpec((SEQ, TC_FOURIER), lambda b, c: (b, c)),
        out_shape=jax.ShapeDtypeStruct((batch * SEQ, FOURIER_WIDTH), BF16),
        compiler_params=pltpu.CompilerParams(
            dimension_semantics=("arbitrary", "arbitrary"), vmem_limit_bytes=VMEM_LIMIT_BYTES),
        name="fourier_mix",
    )(tables, a, m)


def _out_proj_kernel(yf_ref, yg_ref, x_ref, w_ref, g_ref, h_ref, xn_ref):
    h = x_ref[...] + _dot(yf_ref[...], w_ref[:FOURIER_WIDTH, :]) + _dot(yg_ref[...], w_ref[FOURIER_WIDTH:, :])
    h_ref[...] = h
    xn_ref[...] = (h * _rms_scale(h) * g_ref[...]).astype(BF16)


def _out_proj(yf, yg, x2, w_out, norm_g):
    rows = x2.shape[0]
    return pl.pallas_call(
        _out_proj_kernel,
        grid=(rows // TM_OUT,),
        in_specs=[
            pl.BlockSpec((TM_OUT, FOURIER_WIDTH), lambda i: (i, 0)),
            pl.BlockSpec((TM_OUT, GMLP_WIDTH), lambda i: (i, 0)),
            pl.BlockSpec((TM_OUT, D_MODEL), lambda i: (i, 0)),
            pl.BlockSpec((D_MODEL, D_MODEL), lambda i: (0, 0), pipeline_mode=pl.Buffered(1)),
            pl.BlockSpec((1, D_MODEL), lambda i: (0, 0), pipeline_mode=pl.Buffered(1)),
        ],
        out_specs=[
            pl.BlockSpec((TM_OUT, D_MODEL), lambda i: (i, 0)),
            pl.BlockSpec((TM_OUT, D_MODEL), lambda i: (i, 0)),
        ],
        out_shape=[
            jax.ShapeDtypeStruct((rows, D_MODEL), F32),
            jax.ShapeDtypeStruct((rows, D_MODEL), BF16),
        ],
        compiler_params=pltpu.CompilerParams(
            dimension_semantics=("arbitrary",), vmem_limit_bytes=VMEM_LIMIT_BYTES),
        name="out_proj",
    )(yf, yg, x2, w_out, norm_g)


def _mlp_kernel(xn_ref, h_ref, wu_ref, wd_ref, g_ref, o_ref):
    f = pl.program_id(1)

    @pl.when(f == 0)
    def _():
        o_ref[...] = h_ref[...]

    for rb in range(TM_MLP // SUB_MLP):
        rows = slice(rb * SUB_MLP, (rb + 1) * SUB_MLP)
        r = jnp.maximum(_dot(xn_ref[rows, :], wu_ref[0]), 0.0)
        o_ref[rows, :] += _dot((r * r).astype(BF16), wd_ref[...])

    @pl.when(f == pl.num_programs(1) - 1)
    def _():
        h = o_ref[...]
        o_ref[...] = h * _rms_scale(h) * g_ref[...]


def _mlp(xn, h, w_up, w_down, norm_g):
    rows = xn.shape[0]
    return pl.pallas_call(
        _mlp_kernel,
        grid=(rows // TM_MLP, D_FF // TF_MLP),
        in_specs=[
            pl.BlockSpec((TM_MLP, D_MODEL), lambda i, f: (i, 0)),
            pl.BlockSpec((TM_MLP, D_MODEL), lambda i, f: (i, 0)),
            pl.BlockSpec((1, D_MODEL, TF_MLP), lambda i, f: (f, 0, 0)),
            pl.BlockSpec((TF_MLP, D_MODEL), lambda i, f: (f, 0)),
            pl.BlockSpec((1, D_MODEL), lambda i, f: (0, 0), pipeline_mode=pl.Buffered(1)),
        ],
        out_specs=pl.BlockSpec((TM_MLP, D_MODEL), lambda i, f: (i, 0)),
        out_shape=jax.ShapeDtypeStruct((rows, D_MODEL), F32),
        compiler_params=pltpu.CompilerParams(
            dimension_semantics=("arbitrary", "arbitrary"), vmem_limit_bytes=VMEM_LIMIT_BYTES),
        name="mlp",
    )(xn, h, w_up, w_down, norm_g)


def kernel(x, norm_mix_g, w_in, fourier_w, gmlp_v_g, gmlp_ws, gmlp_b, w_out,
           norm_mlp_g, w_up, w_down, norm_final_g):
    batch, seq, d = x.shape
    assert (seq, d) == (SEQ, D_MODEL)
    x2 = x.reshape(batch * seq, d)
    bs = jnp.broadcast_to(gmlp_b[:, :, None], (N_HEADS, CHUNK, GROUP_DIM))

    m = _fold_fourier_weights(fourier_w)
    a, yg, w_up_b, w_down_b, w_out_b = _in_proj(
        x2, norm_mix_g.reshape(1, d), w_in.astype(BF16), gmlp_v_g, gmlp_ws.astype(BF16), bs,
        w_up, w_down, w_out)
    yf = _fourier(a, m, batch)
    h, xn = _out_proj(yf, yg, x2, w_out_b, norm_mlp_g.reshape(1, d))
    y = _mlp(xn, h, w_up_b, w_down_b, norm_final_g.reshape(1, d))
    return y.reshape(batch, seq, d)
```

```python
import functools

import numpy as np
import jax
import jax.numpy as jnp
from jax import lax
from jax.experimental import pallas as pl
from jax.experimental.pallas import tpu as pltpu

D_MODEL = 2048
SEQ = 2048
GROUP_DIM = 128
FOURIER_WIDTH = 1024
GMLP_WIDTH = 1024
N_GROUPS = FOURIER_WIDTH // GROUP_DIM
N_HEADS = GMLP_WIDTH // GROUP_DIM
IN_PROJ_WIDTH = FOURIER_WIDTH + 2 * GMLP_WIDTH
CHUNK = 128
D_FF = 4 * D_MODEL
EPS = 1e-6

F32 = jnp.float32
BF16 = jnp.bfloat16

VMEM_LIMIT_BYTES = 56 * 1024 * 1024

TM_IN = 512
HEADS_PER_DOT = 4
RADIX = 4
TC_FOURIER = 512
TM_OUT = 512
SUB_OUT = 256
TM_MLP = 1024
TF_MLP = 1024
SUB_MLP = 512


def _dft_tables(n):
    k = np.arange(n, dtype=np.int64)
    ang = 2.0 * np.pi * ((k[:, None] * k[None, :]) % n).astype(np.float64) / n
    return np.cos(ang).astype(np.float32), np.sin(ang).astype(np.float32)


def _dot(a, b):
    return jnp.dot(a, b, preferred_element_type=F32)


def _rms_scale(v):
    return lax.rsqrt(jnp.mean(v * v, axis=-1, keepdims=True) + EPS)


def _fold_kernel(cc_ref, sc_ref, wf_ref, m_ref):
    scale = 1.0 / np.sqrt(float(SEQ * GROUP_DIM))
    for g in range(N_GROUPS):
        w = wf_ref[g]
        m1 = jnp.dot(cc_ref[...], w, preferred_element_type=F32, precision=lax.Precision.HIGHEST)
        m2 = jnp.dot(sc_ref[...], w, preferred_element_type=F32, precision=lax.Precision.HIGHEST)
        m_ref[g, :GROUP_DIM, :] = (m1 * scale).astype(BF16)
        m_ref[g, GROUP_DIM:, :] = (m2 * (-scale)).astype(BF16)


def _fold_fourier_weights(fourier_w):
    cc, sc = _dft_tables(GROUP_DIM)
    return pl.pallas_call(
        _fold_kernel,
        out_shape=jax.ShapeDtypeStruct((N_GROUPS, 2 * GROUP_DIM, GROUP_DIM), BF16),
        name="fold_fourier_w",
    )(jnp.asarray(cc), jnp.asarray(sc), fourier_w)


def _gelu(z):
    return 0.5 * z * (1.0 + lax.erf(z * np.float32(np.sqrt(0.5))))


def _in_proj_kernel(x_ref, g_ref, w_ref, gv_ref, ws_ref, bs_ref, wu_ref, wd_ref, wo_ref,
                    a_ref, yg_ref, wu_o_ref, wd_o_ref, wo_o_ref, pf_ref):
    wu_o_ref[0] = wu_ref[...].astype(BF16)
    wd_o_ref[...] = wd_ref[...].astype(BF16)
    wo_o_ref[...] = wo_ref[...].astype(BF16)

    x = x_ref[...]
    xn = (x * _rms_scale(x) * g_ref[...]).astype(BF16)
    pf = _dot(xn, w_ref[:, :FOURIER_WIDTH])
    for g in range(N_GROUPS):
        pf_ref[g] = pf[:, g * GROUP_DIM:(g + 1) * GROUP_DIM]
    for r in range(RADIX):
        for g in range(N_GROUPS):
            a_ref[0, r, :, g * GROUP_DIM:(g + 1) * GROUP_DIM] = (
                pf_ref[g, pl.ds(r, x.shape[0] // RADIX, stride=RADIX), :].astype(BF16))
    n_chunks = x.shape[0] // CHUNK
    width = HEADS_PER_DOT * GROUP_DIM
    for hb in range(N_HEADS // HEADS_PER_DOT):
        cu = FOURIER_WIDTH + hb * width
        cv = cu + GMLP_WIDTH
        u = _gelu(_dot(xn, w_ref[:, cu:cu + width]))
        v = _gelu(_dot(xn, w_ref[:, cv:cv + width]))
        for j in range(HEADS_PER_DOT):
            h = hb * HEADS_PER_DOT + j
            cols = slice(j * GROUP_DIM, (j + 1) * GROUP_DIM)
            vh = v[:, cols]
            vn = (vh * _rms_scale(vh) * gv_ref[h:h + 1, :]).astype(BF16)
            vcat = jnp.concatenate(
                [vn[c * CHUNK:(c + 1) * CHUNK, :] for c in range(n_chunks)], axis=1)
            s = _dot(ws_ref[h], vcat)
            for c in range(n_chunks):
                rows = slice(c * CHUNK, (c + 1) * CHUNK)
                sc = s[:, c * GROUP_DIM:(c + 1) * GROUP_DIM] + bs_ref[h]
                yg_ref[rows, h * GROUP_DIM:(h + 1) * GROUP_DIM] = (u[rows, cols] * sc).astype(BF16)


def _in_proj(x2, norm_g, w_in, gv, ws, bs, w_up, w_down, w_out):
    rows = x2.shape[0]
    steps = rows // TM_IN
    ff_cols = D_FF // steps
    out_rows = D_MODEL // steps
    per_tile = TF_MLP // ff_cols
    assert ff_cols * steps == D_FF and out_rows * steps == D_MODEL and per_tile * ff_cols == TF_MLP
    tiles_per_seq = SEQ // TM_IN
    res_rows = TM_IN // RADIX
    const = lambda *shape: pl.BlockSpec(shape, lambda i: (0,) * len(shape),
                                        pipeline_mode=pl.Buffered(1))
    return pl.pallas_call(
        _in_proj_kernel,
        grid=(steps,),
        in_specs=[
            pl.BlockSpec((TM_IN, D_MODEL), lambda i: (i, 0)),
            const(1, D_MODEL),
            const(D_MODEL, IN_PROJ_WIDTH),
            const(N_HEADS, GROUP_DIM),
            const(N_HEADS, CHUNK, CHUNK),
            const(N_HEADS, CHUNK, GROUP_DIM),
            pl.BlockSpec((D_MODEL, ff_cols), lambda i: (0, i)),
            pl.BlockSpec((ff_cols, D_MODEL), lambda i: (i, 0)),
            pl.BlockSpec((out_rows, D_MODEL), lambda i: (i, 0)),
        ],
        out_specs=[
            pl.BlockSpec((1, RADIX, res_rows, FOURIER_WIDTH),
                         lambda i: (i // tiles_per_seq, 0, i % tiles_per_seq, 0)),
            pl.BlockSpec((TM_IN, GMLP_WIDTH), lambda i: (i, 0)),
            pl.BlockSpec((1, D_MODEL, ff_cols), lambda i: (i // per_tile, 0, i % per_tile)),
            pl.BlockSpec((ff_cols, D_MODEL), lambda i: (i, 0)),
            pl.BlockSpec((out_rows, D_MODEL), lambda i: (i, 0)),
        ],
        out_shape=[
            jax.ShapeDtypeStruct((rows // SEQ, RADIX, SEQ // RADIX, FOURIER_WIDTH), BF16),
            jax.ShapeDtypeStruct((rows, GMLP_WIDTH), BF16),
            jax.ShapeDtypeStruct((D_FF // TF_MLP, D_MODEL, TF_MLP), BF16),
            jax.ShapeDtypeStruct((D_FF, D_MODEL), BF16),
            jax.ShapeDtypeStruct((D_MODEL, D_MODEL), BF16),
        ],
        scratch_shapes=[pltpu.VMEM((N_GROUPS, TM_IN, GROUP_DIM), F32)],
        compiler_params=pltpu.CompilerParams(
            dimension_semantics=("arbitrary",), vmem_limit_bytes=VMEM_LIMIT_BYTES),
        name="in_proj_gmlp",
    )(x2, norm_g, w_in, gv, ws, bs, w_up, w_down, w_out)


def _radix_tables():
    q = SEQ // RADIX
    kp = np.arange(q, dtype=np.int64)[:, None]
    m = np.arange(q, dtype=np.int64)[None, :]
    out = np.empty((RADIX, 2, q, q), np.float32)
    for r in range(RADIX):
        ang = 2.0 * np.pi * ((kp * (RADIX * m + r)) % SEQ).astype(np.float64) / SEQ
        out[r, 0] = np.cos(ang)
        out[r, 1] = np.sin(ang)
    return out


def _fourier_kernel(t_ref, a_ref, m_ref, yf_ref):
    q = SEQ // RADIX
    pc = [_dot(t_ref[r, 0], a_ref[0, r]) for r in range(RADIX)]
    ps = [_dot(t_ref[r, 1], a_ref[0, r]) for r in range(RADIX)]
    ac, bc, cc, dc = pc[0] + pc[2], pc[0] - pc[2], pc[1] + pc[3], pc[1] - pc[3]
    as_, bs, cs, ds = ps[0] + ps[2], ps[0] - ps[2], ps[1] + ps[3], ps[1] - ps[3]
    out_c = (ac + cc, bc - ds, ac - cc, bc + ds)
    out_s = (as_ + cs, bs + dc, as_ - cs, bs - dc)
    for j in range(RADIX):
        c = out_c[j].astype(BF16)
        s = out_s[j].astype(BF16)
        for g in range(m_ref.shape[0]):
            cols = slice(g * GROUP_DIM, (g + 1) * GROUP_DIM)
            y = _dot(jnp.concatenate([c[:, cols], s[:, cols]], axis=1), m_ref[g])
            yf_ref[j * q:(j + 1) * q, cols] = y.astype(BF16)


def _fourier(a, m, batch):
    assert RADIX == 4
    q = SEQ // RADIX
    tables = jnp.asarray(_radix_tables()).astype(BF16)
    groups = TC_FOURIER // GROUP_DIM
    return pl.pallas_call(
        _fourier_kernel,
        grid=(batch, FOURIER_WIDTH // TC_FOURIER),
        in_specs=[
            pl.BlockSpec((RADIX, 2, q, q), lambda b, c: (0, 0, 0, 0), pipeline_mode=pl.Buffered(1)),
            pl.BlockSpec((1, RADIX, q, TC_FOURIER), lambda b, c: (b, 0, 0, c)),
            pl.BlockSpec((groups, 2 * GROUP_DIM, GROUP_DIM), lambda b, c: (c, 0, 0)),
        ],
        out_specs=pl.BlockSpec((SEQ, TC_FOURIER), lambda b, c: (b, c)),
        out_shape=jax.ShapeDtypeStruct((batch * SEQ, FOURIER_WIDTH), BF16),
        compiler_params=pltpu.CompilerParams(
            dimension_semantics=("arbitrary", "arbitrary"), vmem_limit_bytes=VMEM_LIMIT_BYTES),
        name="fourier_mix",
    )(tables, a, m)


def _out_proj_kernel(yf_ref, yg_ref, x_ref, w_ref, g_ref, h_ref, xn_ref):
    for rb in range(TM_OUT // SUB_OUT):
        rows = slice(rb * SUB_OUT, (rb + 1) * SUB_OUT)
        h = (x_ref[rows, :] + _dot(yf_ref[rows, :], w_ref[:FOURIER_WIDTH, :])
             + _dot(yg_ref[rows, :], w_ref[FOURIER_WIDTH:, :]))
        h_ref[rows, :] = h
        xn_ref[rows, :] = (h * _rms_scale(h) * g_ref[...]).astype(BF16)


def _out_proj(yf, yg, x2, w_out, norm_g):
    rows = x2.shape[0]
    return pl.pallas_call(
        _out_proj_kernel,
        grid=(rows // TM_OUT,),
        in_specs=[
            pl.BlockSpec((TM_OUT, FOURIER_WIDTH), lambda i: (i, 0)),
            pl.BlockSpec((TM_OUT, GMLP_WIDTH), lambda i: (i, 0)),
            pl.BlockSpec((TM_OUT, D_MODEL), lambda i: (i, 0)),
            pl.BlockSpec((D_MODEL, D_MODEL), lambda i: (0, 0), pipeline_mode=pl.Buffered(1)),
            pl.BlockSpec((1, D_MODEL), lambda i: (0, 0), pipeline_mode=pl.Buffered(1)),
        ],
        out_specs=[
            pl.BlockSpec((TM_OUT, D_MODEL), lambda i: (i, 0)),
            pl.BlockSpec((TM_OUT, D_MODEL), lambda i: (i, 0)),
        ],
        out_shape=[
            jax.ShapeDtypeStruct((rows, D_MODEL), F32),
            jax.ShapeDtypeStruct((rows, D_MODEL), BF16),
        ],
        compiler_params=pltpu.CompilerParams(
            dimension_semantics=("arbitrary",), vmem_limit_bytes=VMEM_LIMIT_BYTES),
        name="out_proj",
    )(yf, yg, x2, w_out, norm_g)


def _mlp_kernel(xn_ref, h_hbm, wu_ref, wd_ref, g_ref, o_ref, h_buf, h_sem):
    i = pl.program_id(0)
    f = pl.program_id(1)
    last = pl.num_programs(1) - 1

    def h_copy():
        return pltpu.make_async_copy(h_hbm.at[pl.ds(i * TM_MLP, TM_MLP), :], h_buf, h_sem)

    def mlp_rows(rb):
        rows = slice(rb * SUB_MLP, (rb + 1) * SUB_MLP)
        r = jnp.maximum(_dot(xn_ref[rows, :], wu_ref[0]), 0.0)
        return rows, _dot((r * r).astype(BF16), wd_ref[...])

    @pl.when(f == 0)
    def _():
        h_copy().start()
        for rb in range(TM_MLP // SUB_MLP):
            rows, d = mlp_rows(rb)
            o_ref[rows, :] = d

    @pl.when(jnp.logical_and(f > 0, f < last))
    def _():
        for rb in range(TM_MLP // SUB_MLP):
            rows, d = mlp_rows(rb)
            o_ref[rows, :] += d

    @pl.when(f == last)
    def _():
        h_copy().wait()
        for rb in range(TM_MLP // SUB_MLP):
            rows, d = mlp_rows(rb)
            t = h_buf[rows, :] + (o_ref[rows, :] + d)
            o_ref[rows, :] = t * _rms_scale(t) * g_ref[...]


def _mlp(xn, h, w_up, w_down, norm_g):
    rows = xn.shape[0]
    assert D_FF // TF_MLP >= 2
    return pl.pallas_call(
        _mlp_kernel,
        grid=(rows // TM_MLP, D_FF // TF_MLP),
        in_specs=[
            pl.BlockSpec((TM_MLP, D_MODEL), lambda i, f: (i, 0)),
            pl.BlockSpec(memory_space=pl.ANY),
            pl.BlockSpec((1, D_MODEL, TF_MLP), lambda i, f: (f, 0, 0)),
            pl.BlockSpec((TF_MLP, D_MODEL), lambda i, f: (f, 0)),
            pl.BlockSpec((1, D_MODEL), lambda i, f: (0, 0), pipeline_mode=pl.Buffered(1)),
        ],
        out_specs=pl.BlockSpec((TM_MLP, D_MODEL), lambda i, f: (i, 0)),
        out_shape=jax.ShapeDtypeStruct((rows, D_MODEL), F32),
        scratch_shapes=[pltpu.VMEM((TM_MLP, D_MODEL), F32), pltpu.SemaphoreType.DMA(())],
        compiler_params=pltpu.CompilerParams(
            dimension_semantics=("arbitrary", "arbitrary"), vmem_limit_bytes=VMEM_LIMIT_BYTES),
        name="mlp",
    )(xn, h, w_up, w_down, norm_g)


def kernel(x, norm_mix_g, w_in, fourier_w, gmlp_v_g, gmlp_ws, gmlp_b, w_out,
           norm_mlp_g, w_up, w_down, norm_final_g):
    batch, seq, d = x.shape
    assert (seq, d) == (SEQ, D_MODEL)
    x2 = x.reshape(batch * seq, d)
    bs = jnp.broadcast_to(gmlp_b[:, :, None], (N_HEADS, CHUNK, GROUP_DIM))

    m = _fold_fourier_weights(fourier_w)
    a, yg, w_up_b, w_down_b, w_out_b = _in_proj(
        x2, norm_mix_g.reshape(1, d), w_in.astype(BF16), gmlp_v_g, gmlp_ws.astype(BF16), bs,
        w_up, w_down, w_out)
    yf = _fourier(a, m, batch)
    h, xn = _out_proj(yf, yg, x2, w_out_b, norm_mlp_g.reshape(1, d))
    y = _mlp(xn, h, w_up_b, w_down_b, norm_final_g.reshape(1, d))
    return y.reshape(batch, seq, d)
```

```python
import functools

import numpy as np
import jax
import jax.numpy as jnp
from jax import lax
from jax.experimental import pallas as pl
from jax.experimental.pallas import tpu as pltpu

D_MODEL = 2048
SEQ = 2048
GROUP_DIM = 128
FOURIER_WIDTH = 1024
GMLP_WIDTH = 1024
N_GROUPS = FOURIER_WIDTH // GROUP_DIM
N_HEADS = GMLP_WIDTH // GROUP_DIM
IN_PROJ_WIDTH = FOURIER_WIDTH + 2 * GMLP_WIDTH
CHUNK = 128
D_FF = 4 * D_MODEL
EPS = 1e-6

F32 = jnp.float32
BF16 = jnp.bfloat16

VMEM_LIMIT_BYTES = 56 * 1024 * 1024

TM_IN = 512
HEADS_PER_DOT = 4
RADIX = 4
TC_FOURIER = 512
TM_OUT = 512
TM_MLP = 1024
TF_MLP = 1024
SUB_MLP = 512


def _dft_tables(n):
    k = np.arange(n, dtype=np.int64)
    ang = 2.0 * np.pi * ((k[:, None] * k[None, :]) % n).astype(np.float64) / n
    return np.cos(ang).astype(np.float32), np.sin(ang).astype(np.float32)


def _dot(a, b):
    return jnp.dot(a, b, preferred_element_type=F32)


def _rms_scale(v):
    return lax.rsqrt(jnp.mean(v * v, axis=-1, keepdims=True) + EPS)


def _fold_kernel(cc_ref, sc_ref, wf_ref, m_ref):
    scale = 1.0 / np.sqrt(float(SEQ * GROUP_DIM))
    for g in range(N_GROUPS):
        w = wf_ref[g]
        m1 = jnp.dot(cc_ref[...], w, preferred_element_type=F32, precision=lax.Precision.HIGHEST)
        m2 = jnp.dot(sc_ref[...], w, preferred_element_type=F32, precision=lax.Precision.HIGHEST)
        m_ref[g, :GROUP_DIM, :] = (m1 * scale).astype(BF16)
        m_ref[g, GROUP_DIM:, :] = (m2 * (-scale)).astype(BF16)


def _fold_fourier_weights(fourier_w):
    cc, sc = _dft_tables(GROUP_DIM)
    return pl.pallas_call(
        _fold_kernel,
        out_shape=jax.ShapeDtypeStruct((N_GROUPS, 2 * GROUP_DIM, GROUP_DIM), BF16),
        name="fold_fourier_w",
    )(jnp.asarray(cc), jnp.asarray(sc), fourier_w)


def _gelu(z):
    return 0.5 * z * (1.0 + lax.erf(z * np.float32(np.sqrt(0.5))))


def _in_proj_kernel(x_ref, g_ref, w_ref, gv_ref, ws_ref, bs_ref, wu_ref, wd_ref, wo_ref, gmlp_ref,
                    a_ref, yg_ref, wu_o_ref, wd_o_ref, wo_o_ref, pf_ref):
    x = x_ref[...]
    xn = (x * _rms_scale(x) * g_ref[...]).astype(BF16)
    n_chunks = x.shape[0] // CHUNK
    width = HEADS_PER_DOT * GROUP_DIM
    n_blocks = N_HEADS // HEADS_PER_DOT

    def proj_u(hb):
        cu = FOURIER_WIDTH + hb * width
        return _gelu(_dot(xn, w_ref[:, cu:cu + width]))

    def proj_v(hb):
        cv = FOURIER_WIDTH + GMLP_WIDTH + hb * width
        return _gelu(_dot(xn, w_ref[:, cv:cv + width]))

    def gate(hb, u, v):
        for j in range(HEADS_PER_DOT):
            h = hb * HEADS_PER_DOT + j
            cols = slice(j * GROUP_DIM, (j + 1) * GROUP_DIM)
            vh = v[:, cols]
            vn = (vh * _rms_scale(vh) * gv_ref[h:h + 1, :]).astype(BF16)
            vcat = jnp.concatenate(
                [vn[c * CHUNK:(c + 1) * CHUNK, :] for c in range(n_chunks)], axis=1)
            s = _dot(ws_ref[h], vcat)
            for c in range(n_chunks):
                rows = slice(c * CHUNK, (c + 1) * CHUNK)
                sc = s[:, c * GROUP_DIM:(c + 1) * GROUP_DIM] + bs_ref[h]
                yg_ref[rows, h * GROUP_DIM:(h + 1) * GROUP_DIM] = (u[rows, cols] * sc).astype(BF16)

    def fourier_cols(part, n_parts):
        groups = N_GROUPS // n_parts
        c0 = part * groups * GROUP_DIM
        pf = _dot(xn, w_ref[:, c0:c0 + groups * GROUP_DIM])
        for gl in range(groups):
            pf_ref[part * groups + gl] = pf[:, gl * GROUP_DIM:(gl + 1) * GROUP_DIM]
        for r in range(RADIX):
            for gl in range(groups):
                g = part * groups + gl
                a_ref[0, r, :, g * GROUP_DIM:(g + 1) * GROUP_DIM] = (
                    pf_ref[g, pl.ds(r, x.shape[0] // RADIX, stride=RADIX), :].astype(BF16))

    assert n_blocks == 2
    u0, v0 = proj_u(0), proj_v(0)
    u1 = proj_u(1)
    gate(0, u0, v0)
    v1 = proj_v(1)
    for c in range(wu_ref.shape[1] // GROUP_DIM):
        cols = slice(c * GROUP_DIM, (c + 1) * GROUP_DIM)
        wu_o_ref[0, :, cols] = (wu_ref[:, cols] * gmlp_ref[...]).astype(BF16)
    wd_o_ref[...] = wd_ref[...].astype(BF16)
    wo_o_ref[...] = wo_ref[...].astype(BF16)
    fourier_cols(0, 2)
    gate(1, u1, v1)
    fourier_cols(1, 2)


def _in_proj(x2, norm_g, w_in, gv, ws, bs, w_up, w_down, w_out, gmlp):
    rows = x2.shape[0]
    steps = rows // TM_IN
    ff_cols = D_FF // steps
    out_rows = D_MODEL // steps
    per_tile = TF_MLP // ff_cols
    assert ff_cols * steps == D_FF and out_rows * steps == D_MODEL and per_tile * ff_cols == TF_MLP
    tiles_per_seq = SEQ // TM_IN
    res_rows = TM_IN // RADIX
    const = lambda *shape: pl.BlockSpec(shape, lambda i: (0,) * len(shape),
                                        pipeline_mode=pl.Buffered(1))
    return pl.pallas_call(
        _in_proj_kernel,
        grid=(steps,),
        in_specs=[
            pl.BlockSpec((TM_IN, D_MODEL), lambda i: (i, 0)),
            const(1, D_MODEL),
            const(D_MODEL, IN_PROJ_WIDTH),
            const(N_HEADS, GROUP_DIM),
            const(N_HEADS, CHUNK, CHUNK),
            const(N_HEADS, CHUNK, GROUP_DIM),
            pl.BlockSpec((D_MODEL, ff_cols), lambda i: (0, i)),
            pl.BlockSpec((ff_cols, D_MODEL), lambda i: (i, 0)),
            pl.BlockSpec((out_rows, D_MODEL), lambda i: (i, 0)),
            const(D_MODEL, GROUP_DIM),
        ],
        out_specs=[
            pl.BlockSpec((1, RADIX, res_rows, FOURIER_WIDTH),
                         lambda i: (i // tiles_per_seq, 0, i % tiles_per_seq, 0)),
            pl.BlockSpec((TM_IN, GMLP_WIDTH), lambda i: (i, 0)),
            pl.BlockSpec((1, D_MODEL, ff_cols), lambda i: (i // per_tile, 0, i % per_tile)),
            pl.BlockSpec((ff_cols, D_MODEL), lambda i: (i, 0)),
            pl.BlockSpec((out_rows, D_MODEL), lambda i: (i, 0)),
        ],
        out_shape=[
            jax.ShapeDtypeStruct((rows // SEQ, RADIX, SEQ // RADIX, FOURIER_WIDTH), BF16),
            jax.ShapeDtypeStruct((rows, GMLP_WIDTH), BF16),
            jax.ShapeDtypeStruct((D_FF // TF_MLP, D_MODEL, TF_MLP), BF16),
            jax.ShapeDtypeStruct((D_FF, D_MODEL), BF16),
            jax.ShapeDtypeStruct((D_MODEL, D_MODEL), BF16),
        ],
        scratch_shapes=[pltpu.VMEM((N_GROUPS, TM_IN, GROUP_DIM), F32)],
        compiler_params=pltpu.CompilerParams(
            dimension_semantics=("arbitrary",), vmem_limit_bytes=VMEM_LIMIT_BYTES),
        name="in_proj_gmlp",
    )(x2, norm_g, w_in, gv, ws, bs, w_up, w_down, w_out, gmlp)


def _radix_tables():
    q = SEQ // RADIX
    kp = np.arange(q, dtype=np.int64)[:, None]
    m = np.arange(q, dtype=np.int64)[None, :]
    out = np.empty((RADIX, 2, q, q), np.float32)
    for r in range(RADIX):
        ang = 2.0 * np.pi * ((kp * (RADIX * m + r)) % SEQ).astype(np.float64) / SEQ
        out[r, 0] = np.cos(ang)
        out[r, 1] = np.sin(ang)
    return out


def _fourier_kernel(t_ref, a_ref, m_ref, yf_ref):
    q = SEQ // RADIX
    pc = [_dot(t_ref[r, 0], a_ref[0, r]) for r in range(RADIX)]
    ps = [_dot(t_ref[r, 1], a_ref[0, r]) for r in range(RADIX)]
    ac, bc, cc, dc = pc[0] + pc[2], pc[0] - pc[2], pc[1] + pc[3], pc[1] - pc[3]
    as_, bs, cs, ds = ps[0] + ps[2], ps[0] - ps[2], ps[1] + ps[3], ps[1] - ps[3]
    out_c = (ac + cc, bc - ds, ac - cc, bc + ds)
    out_s = (as_ + cs, bs + dc, as_ - cs, bs - dc)
    for j in range(RADIX):
        c = out_c[j].astype(BF16)
        s = out_s[j].astype(BF16)
        for g in range(m_ref.shape[0]):
            cols = slice(g * GROUP_DIM, (g + 1) * GROUP_DIM)
            y = _dot(jnp.concatenate([c[:, cols], s[:, cols]], axis=1), m_ref[g])
            yf_ref[j * q:(j + 1) * q, cols] = y.astype(BF16)


def _fourier(a, m, batch):
    assert RADIX == 4
    q = SEQ // RADIX
    tables = jnp.asarray(_radix_tables()).astype(BF16)
    groups = TC_FOURIER // GROUP_DIM
    return pl.pallas_call(
        _fourier_kernel,
        grid=(batch, FOURIER_WIDTH // TC_FOURIER),
        in_specs=[
            pl.BlockSpec((RADIX, 2, q, q), lambda b, c: (0, 0, 0, 0), pipeline_mode=pl.Buffered(1)),
            pl.BlockSpec((1, RADIX, q, TC_FOURIER), lambda b, c: (b, 0, 0, c)),
            pl.BlockSpec((groups, 2 * GROUP_DIM, GROUP_DIM), lambda b, c: (c, 0, 0)),
        ],
        out_specs=pl.BlockSpec((SEQ, TC_FOURIER), lambda b, c: (b, c)),
        out_shape=jax.ShapeDtypeStruct((batch * SEQ, FOURIER_WIDTH), BF16),
        compiler_params=pltpu.CompilerParams(
            dimension_semantics=("arbitrary", "arbitrary"), vmem_limit_bytes=VMEM_LIMIT_BYTES),
        name="fourier_mix",
    )(tables, a, m)


def _out_proj_kernel(yf_ref, yg_ref, x_ref, w_ref, h_ref, hb_ref):
    mix = jnp.concatenate([yf_ref[...], yg_ref[...]], axis=1)
    h = x_ref[...] + _dot(mix, w_ref[...])
    h_ref[...] = h
    hb_ref[...] = h.astype(BF16)


def _out_proj(yf, yg, x2, w_out):
    rows = x2.shape[0]
    return pl.pallas_call(
        _out_proj_kernel,
        grid=(rows // TM_OUT,),
        in_specs=[
            pl.BlockSpec((TM_OUT, FOURIER_WIDTH), lambda i: (i, 0)),
            pl.BlockSpec((TM_OUT, GMLP_WIDTH), lambda i: (i, 0)),
            pl.BlockSpec((TM_OUT, D_MODEL), lambda i: (i, 0)),
            pl.BlockSpec((D_MODEL, D_MODEL), lambda i: (0, 0), pipeline_mode=pl.Buffered(1)),
        ],
        out_specs=[
            pl.BlockSpec((TM_OUT, D_MODEL), lambda i: (i, 0)),
            pl.BlockSpec((TM_OUT, D_MODEL), lambda i: (i, 0)),
        ],
        out_shape=[
            jax.ShapeDtypeStruct((rows, D_MODEL), F32),
            jax.ShapeDtypeStruct((rows, D_MODEL), BF16),
        ],
        compiler_params=pltpu.CompilerParams(
            dimension_semantics=("arbitrary",), vmem_limit_bytes=VMEM_LIMIT_BYTES),
        name="out_proj",
    )(yf, yg, x2, w_out)


def _mlp_kernel(hb_ref, h_hbm, wu_ref, wd_ref, g_ref, o_ref, h_buf, h_sem):
    i = pl.program_id(0)
    f = pl.program_id(1)
    last = pl.num_programs(1) - 1

    def h_copy():
        return pltpu.make_async_copy(h_hbm.at[pl.ds(i * TM_MLP, TM_MLP), :], h_buf, h_sem)

    def mlp_rows(rb):
        rows = slice(rb * SUB_MLP, (rb + 1) * SUB_MLP)
        r = jnp.maximum(_dot(hb_ref[rows, :], wu_ref[0]), 0.0)
        return rows, _dot((r * r).astype(BF16), wd_ref[...])

    @pl.when(f == 0)
    def _():
        h_copy().start()
        for rb in range(TM_MLP // SUB_MLP):
            rows, d = mlp_rows(rb)
            o_ref[rows, :] = d

    @pl.when(jnp.logical_and(f > 0, f < last))
    def _():
        for rb in range(TM_MLP // SUB_MLP):
            rows, d = mlp_rows(rb)
            o_ref[rows, :] += d

    @pl.when(f == last)
    def _():
        h_copy().wait()
        for rb in range(TM_MLP // SUB_MLP):
            rows, d = mlp_rows(rb)
            h = h_buf[rows, :]
            s = _rms_scale(h)
            t = h + (o_ref[rows, :] + d) * (s * s)
            o_ref[rows, :] = t * _rms_scale(t) * g_ref[...]


def _mlp(xn, h, w_up, w_down, norm_g):
    rows = xn.shape[0]
    assert D_FF // TF_MLP >= 2
    return pl.pallas_call(
        _mlp_kernel,
        grid=(rows // TM_MLP, D_FF // TF_MLP),
        in_specs=[
            pl.BlockSpec((TM_MLP, D_MODEL), lambda i, f: (i, 0)),
            pl.BlockSpec(memory_space=pl.ANY),
            pl.BlockSpec((1, D_MODEL, TF_MLP), lambda i, f: (f, 0, 0)),
            pl.BlockSpec((TF_MLP, D_MODEL), lambda i, f: (f, 0)),
            pl.BlockSpec((1, D_MODEL), lambda i, f: (0, 0), pipeline_mode=pl.Buffered(1)),
        ],
        out_specs=pl.BlockSpec((TM_MLP, D_MODEL), lambda i, f: (i, 0)),
        out_shape=jax.ShapeDtypeStruct((rows, D_MODEL), F32),
        scratch_shapes=[pltpu.VMEM((TM_MLP, D_MODEL), F32), pltpu.SemaphoreType.DMA(())],
        compiler_params=pltpu.CompilerParams(
            dimension_semantics=("arbitrary", "arbitrary"), vmem_limit_bytes=VMEM_LIMIT_BYTES),
        name="mlp",
    )(xn, h, w_up, w_down, norm_g)


def kernel(x, norm_mix_g, w_in, fourier_w, gmlp_v_g, gmlp_ws, gmlp_b, w_out,
           norm_mlp_g, w_up, w_down, norm_final_g):
    batch, seq, d = x.shape
    assert (seq, d) == (SEQ, D_MODEL)
    x2 = x.reshape(batch * seq, d)
    bs = jnp.broadcast_to(gmlp_b[:, :, None], (N_HEADS, CHUNK, GROUP_DIM))

    m = _fold_fourier_weights(fourier_w)
    a, yg, w_up_b, w_down_b, w_out_b = _in_proj(
        x2, norm_mix_g.reshape(1, d), w_in.astype(BF16), gmlp_v_g, gmlp_ws.astype(BF16), bs,
        w_up, w_down, w_out, jnp.broadcast_to(norm_mlp_g[:, None], (d, GROUP_DIM)))
    yf = _fourier(a, m, batch)
    h, hb = _out_proj(yf, yg, x2, w_out_b)
    y = _mlp(hb, h, w_up_b, w_down_b, norm_final_g.reshape(1, d))
    return y.reshape(batch, seq, d)
```

```python
import numpy as np
import jax
import jax.numpy as jnp
from jax import lax
from jax.experimental import pallas as pl
from jax.experimental.pallas import tpu as pltpu

D_MODEL = 2048
SEQ = 2048
GROUP_DIM = 128
FOURIER_WIDTH = 1024
GMLP_WIDTH = 1024
N_GROUPS = FOURIER_WIDTH // GROUP_DIM
N_HEADS = GMLP_WIDTH // GROUP_DIM
IN_PROJ_WIDTH = FOURIER_WIDTH + 2 * GMLP_WIDTH
CHUNK = 128
D_FF = 4 * D_MODEL
EPS = 1e-6

F32 = jnp.float32
BF16 = jnp.bfloat16

VMEM_LIMIT_BYTES = 56 * 1024 * 1024

TM_IN = 512
HEADS_PER_DOT = 4
RADIX = 4
TC_FOURIER = 512
TM_OUT = 1024
SUB_OUT = 512
TM_MLP = 1024
TF_MLP = 1024
SUB_MLP = 512


def _dft_tables(n):
    k = np.arange(n, dtype=np.int64)
    ang = 2.0 * np.pi * ((k[:, None] * k[None, :]) % n).astype(np.float64) / n
    return np.cos(ang).astype(np.float32), np.sin(ang).astype(np.float32)


def _dot(a, b):
    return jnp.dot(a, b, preferred_element_type=F32)


def _rms_scale(v):
    return lax.rsqrt(jnp.mean(v * v, axis=-1, keepdims=True) + EPS)


def _fold_kernel(cc_ref, sc_ref, wf_ref, m_ref):
    scale = 1.0 / np.sqrt(float(SEQ * GROUP_DIM))
    for g in range(N_GROUPS):
        w = wf_ref[g]
        m1 = jnp.dot(cc_ref[...], w, preferred_element_type=F32, precision=lax.Precision.HIGHEST)
        m2 = jnp.dot(sc_ref[...], w, preferred_element_type=F32, precision=lax.Precision.HIGHEST)
        m_ref[g, :GROUP_DIM, :] = (m1 * scale).astype(BF16)
        m_ref[g, GROUP_DIM:, :] = (m2 * (-scale)).astype(BF16)


def _fold_fourier_weights(fourier_w):
    cc, sc = _dft_tables(GROUP_DIM)
    return pl.pallas_call(
        _fold_kernel,
        out_shape=jax.ShapeDtypeStruct((N_GROUPS, 2 * GROUP_DIM, GROUP_DIM), BF16),
        name="fold_fourier_w",
    )(jnp.asarray(cc), jnp.asarray(sc), fourier_w)


def _gelu(z):
    return 0.5 * z * (1.0 + lax.erf(z * np.float32(np.sqrt(0.5))))


def _in_proj_kernel(x_ref, g_ref, w_ref, gv_ref, ws_ref, bs_ref, wu_ref, wd_ref, wo_ref, gmlp_ref,
                    a_ref, yg_ref, wu_o_ref, wd_o_ref, wo_o_ref, pf_ref):
    x = x_ref[...]
    xn = (x * _rms_scale(x) * g_ref[...]).astype(BF16)
    n_chunks = x.shape[0] // CHUNK
    width = HEADS_PER_DOT * GROUP_DIM
    n_blocks = N_HEADS // HEADS_PER_DOT

    def proj_u(hb):
        cu = FOURIER_WIDTH + hb * width
        return _gelu(_dot(xn, w_ref[:, cu:cu + width]))

    def proj_v(hb):
        cv = FOURIER_WIDTH + GMLP_WIDTH + hb * width
        return _gelu(_dot(xn, w_ref[:, cv:cv + width]))

    def gate(hb, u, v):
        for j in range(HEADS_PER_DOT):
            h = hb * HEADS_PER_DOT + j
            cols = slice(j * GROUP_DIM, (j + 1) * GROUP_DIM)
            vh = v[:, cols]
            vn = (vh * _rms_scale(vh) * gv_ref[h:h + 1, :]).astype(BF16)
            vcat = jnp.concatenate(
                [vn[c * CHUNK:(c + 1) * CHUNK, :] for c in range(n_chunks)], axis=1)
            s = _dot(ws_ref[h], vcat)
            for c in range(n_chunks):
                rows = slice(c * CHUNK, (c + 1) * CHUNK)
                sc = s[:, c * GROUP_DIM:(c + 1) * GROUP_DIM] + bs_ref[h]
                yg_ref[rows, h * GROUP_DIM:(h + 1) * GROUP_DIM] = (u[rows, cols] * sc).astype(BF16)

    def fourier_cols(part, n_parts):
        groups = N_GROUPS // n_parts
        c0 = part * groups * GROUP_DIM
        pf = _dot(xn, w_ref[:, c0:c0 + groups * GROUP_DIM])
        for gl in range(groups):
            pf_ref[part * groups + gl] = pf[:, gl * GROUP_DIM:(gl + 1) * GROUP_DIM]
        for r in range(RADIX):
            for gl in range(groups):
                g = part * groups + gl
                a_ref[0, r, :, g * GROUP_DIM:(g + 1) * GROUP_DIM] = (
                    pf_ref[g, pl.ds(r, x.shape[0] // RADIX, stride=RADIX), :].astype(BF16))

    assert n_blocks == 2
    u0, v0 = proj_u(0), proj_v(0)
    u1 = proj_u(1)
    gate(0, u0, v0)
    v1 = proj_v(1)
    for c in range(wu_ref.shape[1] // GROUP_DIM):
        cols = slice(c * GROUP_DIM, (c + 1) * GROUP_DIM)
        wu_o_ref[0, :, cols] = (wu_ref[:, cols] * gmlp_ref[...]).astype(BF16)
    wd_o_ref[...] = wd_ref[...].astype(BF16)
    wo_o_ref[...] = wo_ref[...].astype(BF16)
    fourier_cols(0, 2)
    gate(1, u1, v1)
    fourier_cols(1, 2)


def _in_proj(x2, norm_g, w_in, gv, ws, bs, w_up, w_down, w_out, gmlp):
    rows = x2.shape[0]
    steps = rows // TM_IN
    ff_cols = D_FF // steps
    out_rows = D_MODEL // steps
    per_tile = TF_MLP // ff_cols
    assert ff_cols * steps == D_FF and out_rows * steps == D_MODEL and per_tile * ff_cols == TF_MLP
    tiles_per_seq = SEQ // TM_IN
    res_rows = TM_IN // RADIX
    const = lambda *shape: pl.BlockSpec(shape, lambda i: (0,) * len(shape),
                                        pipeline_mode=pl.Buffered(1))
    return pl.pallas_call(
        _in_proj_kernel,
        grid=(steps,),
        in_specs=[
            pl.BlockSpec((TM_IN, D_MODEL), lambda i: (i, 0)),
            const(1, D_MODEL),
            const(D_MODEL, IN_PROJ_WIDTH),
            const(N_HEADS, GROUP_DIM),
            const(N_HEADS, CHUNK, CHUNK),
            const(N_HEADS, CHUNK, GROUP_DIM),
            pl.BlockSpec((D_MODEL, ff_cols), lambda i: (0, i)),
            pl.BlockSpec((ff_cols, D_MODEL), lambda i: (i, 0)),
            pl.BlockSpec((out_rows, D_MODEL), lambda i: (i, 0)),
            const(D_MODEL, GROUP_DIM),
        ],
        out_specs=[
            pl.BlockSpec((1, RADIX, res_rows, FOURIER_WIDTH),
                         lambda i: (i // tiles_per_seq, 0, i % tiles_per_seq, 0)),
            pl.BlockSpec((TM_IN, GMLP_WIDTH), lambda i: (i, 0)),
            pl.BlockSpec((1, D_MODEL, ff_cols), lambda i: (i // per_tile, 0, i % per_tile)),
            pl.BlockSpec((ff_cols, D_MODEL), lambda i: (i, 0)),
            pl.BlockSpec((out_rows, D_MODEL), lambda i: (i, 0)),
        ],
        out_shape=[
            jax.ShapeDtypeStruct((rows // SEQ, RADIX, SEQ // RADIX, FOURIER_WIDTH), BF16),
            jax.ShapeDtypeStruct((rows, GMLP_WIDTH), BF16),
            jax.ShapeDtypeStruct((D_FF // TF_MLP, D_MODEL, TF_MLP), BF16),
            jax.ShapeDtypeStruct((D_FF, D_MODEL), BF16),
            jax.ShapeDtypeStruct((D_MODEL, D_MODEL), BF16),
        ],
        scratch_shapes=[pltpu.VMEM((N_GROUPS, TM_IN, GROUP_DIM), F32)],
        compiler_params=pltpu.CompilerParams(
            dimension_semantics=("arbitrary",), vmem_limit_bytes=VMEM_LIMIT_BYTES),
        name="in_proj_gmlp",
    )(x2, norm_g, w_in, gv, ws, bs, w_up, w_down, w_out, gmlp)


def _radix_tables():
    q = SEQ // RADIX
    kp = np.arange(q, dtype=np.int64)[:, None]
    m = np.arange(q, dtype=np.int64)[None, :]
    out = np.empty((RADIX, 2, q, q), np.float32)
    for r in range(RADIX):
        ang = 2.0 * np.pi * ((kp * (RADIX * m + r)) % SEQ).astype(np.float64) / SEQ
        out[r, 0] = np.cos(ang)
        out[r, 1] = np.sin(ang)
    return out


def _fourier_kernel(t_ref, a_ref, m_ref, yf_ref):
    q = SEQ // RADIX
    pc = [_dot(t_ref[r, 0], a_ref[0, r]) for r in range(RADIX)]
    ps = [_dot(t_ref[r, 1], a_ref[0, r]) for r in range(RADIX)]
    ac, bc, cc, dc = pc[0] + pc[2], pc[0] - pc[2], pc[1] + pc[3], pc[1] - pc[3]
    as_, bs, cs, ds = ps[0] + ps[2], ps[0] - ps[2], ps[1] + ps[3], ps[1] - ps[3]
    out_c = (ac + cc, bc - ds, ac - cc, bc + ds)
    out_s = (as_ + cs, bs + dc, as_ - cs, bs - dc)
    for j in range(RADIX):
        c = out_c[j].astype(BF16)
        s = out_s[j].astype(BF16)
        for g in range(m_ref.shape[0]):
            cols = slice(g * GROUP_DIM, (g + 1) * GROUP_DIM)
            y = _dot(jnp.concatenate([c[:, cols], s[:, cols]], axis=1), m_ref[g])
            yf_ref[j * q:(j + 1) * q, cols] = y.astype(BF16)


def _fourier(a, m, batch):
    assert RADIX == 4
    q = SEQ // RADIX
    tables = jnp.asarray(_radix_tables()).astype(BF16)
    groups = TC_FOURIER // GROUP_DIM
    return pl.pallas_call(
        _fourier_kernel,
        grid=(batch, FOURIER_WIDTH // TC_FOURIER),
        in_specs=[
            pl.BlockSpec((RADIX, 2, q, q), lambda b, c: (0, 0, 0, 0), pipeline_mode=pl.Buffered(1)),
            pl.BlockSpec((1, RADIX, q, TC_FOURIER), lambda b, c: (b, 0, 0, c)),
            pl.BlockSpec((groups, 2 * GROUP_DIM, GROUP_DIM), lambda b, c: (c, 0, 0)),
        ],
        out_specs=pl.BlockSpec((SEQ, TC_FOURIER), lambda b, c: (b, c)),
        out_shape=jax.ShapeDtypeStruct((batch * SEQ, FOURIER_WIDTH), BF16),
        compiler_params=pltpu.CompilerParams(
            dimension_semantics=("arbitrary", "arbitrary"), vmem_limit_bytes=VMEM_LIMIT_BYTES),
        name="fourier_mix",
    )(tables, a, m)


def _out_proj_kernel(yf_ref, yg_ref, x_ref, w_ref, h_ref, hb_ref):
    for rb in range(TM_OUT // SUB_OUT):
        rows = slice(rb * SUB_OUT, (rb + 1) * SUB_OUT)
        mix = jnp.concatenate([yf_ref[rows, :], yg_ref[rows, :]], axis=1)
        h = x_ref[rows, :] + _dot(mix, w_ref[...])
        h_ref[rows, :] = h
        hb_ref[rows, :] = h.astype(BF16)


def _out_proj(yf, yg, x2, w_out):
    rows = x2.shape[0]
    return pl.pallas_call(
        _out_proj_kernel,
        grid=(rows // TM_OUT,),
        in_specs=[
            pl.BlockSpec((TM_OUT, FOURIER_WIDTH), lambda i: (i, 0)),
            pl.BlockSpec((TM_OUT, GMLP_WIDTH), lambda i: (i, 0)),
            pl.BlockSpec((TM_OUT, D_MODEL), lambda i: (i, 0)),
            pl.BlockSpec((D_MODEL, D_MODEL), lambda i: (0, 0), pipeline_mode=pl.Buffered(1)),
        ],
        out_specs=[
            pl.BlockSpec((TM_OUT, D_MODEL), lambda i: (i, 0)),
            pl.BlockSpec((TM_OUT, D_MODEL), lambda i: (i, 0)),
        ],
        out_shape=[
            jax.ShapeDtypeStruct((rows, D_MODEL), F32),
            jax.ShapeDtypeStruct((rows, D_MODEL), BF16),
        ],
        compiler_params=pltpu.CompilerParams(
            dimension_semantics=("arbitrary",), vmem_limit_bytes=58 * 1024 * 1024),
        name="out_proj",
    )(yf, yg, x2, w_out)


def _mlp_kernel(hb_ref, h_hbm, wu_ref, wd_ref, g_ref, o_ref, h_buf, h_sem):
    i = pl.program_id(0)
    f = pl.program_id(1)
    last = pl.num_programs(1) - 1

    def h_copy():
        return pltpu.make_async_copy(h_hbm.at[pl.ds(i * TM_MLP, TM_MLP), :], h_buf, h_sem)

    def mlp_rows(rb):
        rows = slice(rb * SUB_MLP, (rb + 1) * SUB_MLP)
        r = jnp.maximum(_dot(hb_ref[rows, :], wu_ref[0]), 0.0)
        return rows, _dot((r * r).astype(BF16), wd_ref[...])

    @pl.when(f == 0)
    def _():
        h_copy().start()
        for rb in range(TM_MLP // SUB_MLP):
            rows, d = mlp_rows(rb)
            o_ref[rows, :] = d

    @pl.when(jnp.logical_and(f > 0, f < last))
    def _():
        for rb in range(TM_MLP // SUB_MLP):
            rows, d = mlp_rows(rb)
            o_ref[rows, :] += d

    @pl.when(f == last)
    def _():
        h_copy().wait()
        for rb in range(TM_MLP // SUB_MLP):
            rows, d = mlp_rows(rb)
            h = h_buf[rows, :]
            s = _rms_scale(h)
            t = h + (o_ref[rows, :] + d) * (s * s)
            o_ref[rows, :] = t * _rms_scale(t) * g_ref[...]


def _mlp(hb, h, w_up, w_down, norm_g):
    rows = hb.shape[0]
    assert D_FF // TF_MLP >= 2
    return pl.pallas_call(
        _mlp_kernel,
        grid=(rows // TM_MLP, D_FF // TF_MLP),
        in_specs=[
            pl.BlockSpec((TM_MLP, D_MODEL), lambda i, f: (i, 0)),
            pl.BlockSpec(memory_space=pl.ANY),
            pl.BlockSpec((1, D_MODEL, TF_MLP), lambda i, f: (f, 0, 0)),
            pl.BlockSpec((TF_MLP, D_MODEL), lambda i, f: (f, 0)),
            pl.BlockSpec((1, D_MODEL), lambda i, f: (0, 0), pipeline_mode=pl.Buffered(1)),
        ],
        out_specs=pl.BlockSpec((TM_MLP, D_MODEL), lambda i, f: (i, 0)),
        out_shape=jax.ShapeDtypeStruct((rows, D_MODEL), F32),
        scratch_shapes=[pltpu.VMEM((TM_MLP, D_MODEL), F32), pltpu.SemaphoreType.DMA(())],
        compiler_params=pltpu.CompilerParams(
            dimension_semantics=("arbitrary", "arbitrary"), vmem_limit_bytes=VMEM_LIMIT_BYTES),
        name="mlp",
    )(hb, h, w_up, w_down, norm_g)


def kernel(x, norm_mix_g, w_in, fourier_w, gmlp_v_g, gmlp_ws, gmlp_b, w_out,
           norm_mlp_g, w_up, w_down, norm_final_g):
    batch, seq, d = x.shape
    assert (seq, d) == (SEQ, D_MODEL)
    x2 = x.reshape(batch * seq, d)
    bs = jnp.broadcast_to(gmlp_b[:, :, None], (N_HEADS, CHUNK, GROUP_DIM))

    m = _fold_fourier_weights(fourier_w)
    a, yg, w_up_b, w_down_b, w_out_b = _in_proj(
        x2, norm_mix_g.reshape(1, d), w_in.astype(BF16), gmlp_v_g, gmlp_ws.astype(BF16), bs,
        w_up, w_down, w_out, jnp.broadcast_to(norm_mlp_g[:, None], (d, GROUP_DIM)))
    yf = _fourier(a, m, batch)
    h, hb = _out_proj(yf, yg, x2, w_out_b)
    y = _mlp(hb, h, w_up_b, w_down_b, norm_final_g.reshape(1, d))
    return y.reshape(batch, seq, d)
```

```python
import numpy as np
import jax
import jax.numpy as jnp
from jax import lax
from jax.experimental import pallas as pl
from jax.experimental.pallas import tpu as pltpu

D_MODEL = 2048
SEQ = 2048
GROUP_DIM = 128
FOURIER_WIDTH = 1024
GMLP_WIDTH = 1024
N_GROUPS = FOURIER_WIDTH // GROUP_DIM
N_HEADS = GMLP_WIDTH // GROUP_DIM
IN_PROJ_WIDTH = FOURIER_WIDTH + 2 * GMLP_WIDTH
CHUNK = 128
D_FF = 4 * D_MODEL
EPS = 1e-6

F32 = jnp.float32
BF16 = jnp.bfloat16

VMEM_LIMIT_BYTES = 56 * 1024 * 1024

TM_IN = 512
HEADS_PER_DOT = 4
RADIX = 4
TC_FOURIER = 1024
TM_OUT = 1024
SUB_OUT = 512
TM_MLP = 1024
TF_MLP = 1024
SUB_MLP = 512


def _dft_tables(n):
    k = np.arange(n, dtype=np.int64)
    ang = 2.0 * np.pi * ((k[:, None] * k[None, :]) % n).astype(np.float64) / n
    return np.cos(ang).astype(np.float32), np.sin(ang).astype(np.float32)


def _dot(a, b):
    return jnp.dot(a, b, preferred_element_type=F32)


def _rms_scale(v):
    return lax.rsqrt(jnp.mean(v * v, axis=-1, keepdims=True) + EPS)


def _fold_kernel(cc_ref, sc_ref, wf_ref, m_ref):
    scale = 1.0 / np.sqrt(float(SEQ * GROUP_DIM))
    for g in range(N_GROUPS):
        w = wf_ref[g]
        m1 = jnp.dot(cc_ref[...], w, preferred_element_type=F32, precision=lax.Precision.HIGHEST)
        m2 = jnp.dot(sc_ref[...], w, preferred_element_type=F32, precision=lax.Precision.HIGHEST)
        m_ref[g, :GROUP_DIM, :] = (m1 * scale).astype(BF16)
        m_ref[g, GROUP_DIM:, :] = (m2 * (-scale)).astype(BF16)


def _fold_fourier_weights(fourier_w):
    cc, sc = _dft_tables(GROUP_DIM)
    return pl.pallas_call(
        _fold_kernel,
        out_shape=jax.ShapeDtypeStruct((N_GROUPS, 2 * GROUP_DIM, GROUP_DIM), BF16),
        name="fold_fourier_w",
    )(jnp.asarray(cc), jnp.asarray(sc), fourier_w)


def _gelu(z):
    return 0.5 * z * (1.0 + lax.erf(z * np.float32(np.sqrt(0.5))))


def _in_proj_kernel(x_ref, g_ref, w_ref, gv_ref, ws_ref, bs_ref, wu_ref, wd_ref, wo_ref, gmlp_ref,
                    a_ref, yg_ref, wu_o_ref, wd_o_ref, wo_o_ref, pf_ref):
    x = x_ref[...]
    xn = (x * _rms_scale(x) * g_ref[...]).astype(BF16)
    n_chunks = x.shape[0] // CHUNK
    width = HEADS_PER_DOT * GROUP_DIM
    n_blocks = N_HEADS // HEADS_PER_DOT

    def proj_u(hb):
        cu = FOURIER_WIDTH + hb * width
        return _gelu(_dot(xn, w_ref[:, cu:cu + width]))

    def proj_v(hb):
        cv = FOURIER_WIDTH + GMLP_WIDTH + hb * width
        return _gelu(_dot(xn, w_ref[:, cv:cv + width]))

    def gate(hb, u, v):
        for j in range(HEADS_PER_DOT):
            h = hb * HEADS_PER_DOT + j
            cols = slice(j * GROUP_DIM, (j + 1) * GROUP_DIM)
            vh = v[:, cols]
            vn = (vh * _rms_scale(vh) * gv_ref[h:h + 1, :]).astype(BF16)
            vcat = jnp.concatenate(
                [vn[c * CHUNK:(c + 1) * CHUNK, :] for c in range(n_chunks)], axis=1)
            s = _dot(ws_ref[h], vcat)
            for c in range(n_chunks):
                rows = slice(c * CHUNK, (c + 1) * CHUNK)
                sc = s[:, c * GROUP_DIM:(c + 1) * GROUP_DIM] + bs_ref[h]
                yg_ref[rows, h * GROUP_DIM:(h + 1) * GROUP_DIM] = (u[rows, cols] * sc).astype(BF16)

    def fourier_cols(part, n_parts):
        groups = N_GROUPS // n_parts
        c0 = part * groups * GROUP_DIM
        pf = _dot(xn, w_ref[:, c0:c0 + groups * GROUP_DIM])
        for gl in range(groups):
            pf_ref[part * groups + gl] = pf[:, gl * GROUP_DIM:(gl + 1) * GROUP_DIM]
        for r in range(RADIX):
            for gl in range(groups):
                g = part * groups + gl
                a_ref[0, r, :, g * GROUP_DIM:(g + 1) * GROUP_DIM] = (
                    pf_ref[g, pl.ds(r, x.shape[0] // RADIX, stride=RADIX), :].astype(BF16))

    assert n_blocks == 2
    u0, v0 = proj_u(0), proj_v(0)
    u1 = proj_u(1)
    gate(0, u0, v0)
    v1 = proj_v(1)
    for c in range(wu_ref.shape[1] // GROUP_DIM):
        cols = slice(c * GROUP_DIM, (c + 1) * GROUP_DIM)
        wu_o_ref[0, :, cols] = (wu_ref[:, cols] * gmlp_ref[...]).astype(BF16)
    wd_o_ref[...] = wd_ref[...].astype(BF16)
    wo_o_ref[...] = wo_ref[...].astype(BF16)
    fourier_cols(0, 2)
    gate(1, u1, v1)
    fourier_cols(1, 2)


def _in_proj(x2, norm_g, w_in, gv, ws, bs, w_up, w_down, w_out, gmlp):
    rows = x2.shape[0]
    steps = rows // TM_IN
    ff_cols = D_FF // steps
    out_rows = D_MODEL // steps
    per_tile = TF_MLP // ff_cols
    assert ff_cols * steps == D_FF and out_rows * steps == D_MODEL and per_tile * ff_cols == TF_MLP
    tiles_per_seq = SEQ // TM_IN
    res_rows = TM_IN // RADIX
    const = lambda *shape: pl.BlockSpec(shape, lambda i: (0,) * len(shape),
                                        pipeline_mode=pl.Buffered(1))
    return pl.pallas_call(
        _in_proj_kernel,
        grid=(steps,),
        in_specs=[
            pl.BlockSpec((TM_IN, D_MODEL), lambda i: (i, 0)),
            const(1, D_MODEL),
            const(D_MODEL, IN_PROJ_WIDTH),
            const(N_HEADS, GROUP_DIM),
            const(N_HEADS, CHUNK, CHUNK),
            const(N_HEADS, CHUNK, GROUP_DIM),
            pl.BlockSpec((D_MODEL, ff_cols), lambda i: (0, i)),
            pl.BlockSpec((ff_cols, D_MODEL), lambda i: (i, 0)),
            pl.BlockSpec((out_rows, D_MODEL), lambda i: (i, 0)),
            const(D_MODEL, GROUP_DIM),
        ],
        out_specs=[
            pl.BlockSpec((1, RADIX, res_rows, FOURIER_WIDTH),
                         lambda i: (i // tiles_per_seq, 0, i % tiles_per_seq, 0)),
            pl.BlockSpec((TM_IN, GMLP_WIDTH), lambda i: (i, 0)),
            pl.BlockSpec((1, D_MODEL, ff_cols), lambda i: (i // per_tile, 0, i % per_tile)),
            pl.BlockSpec((ff_cols, D_MODEL), lambda i: (i, 0)),
            pl.BlockSpec((out_rows, D_MODEL), lambda i: (i, 0)),
        ],
        out_shape=[
            jax.ShapeDtypeStruct((rows // SEQ, RADIX, SEQ // RADIX, FOURIER_WIDTH), BF16),
            jax.ShapeDtypeStruct((rows, GMLP_WIDTH), BF16),
            jax.ShapeDtypeStruct((D_FF // TF_MLP, D_MODEL, TF_MLP), BF16),
            jax.ShapeDtypeStruct((D_FF, D_MODEL), BF16),
            jax.ShapeDtypeStruct((D_MODEL, D_MODEL), BF16),
        ],
        scratch_shapes=[pltpu.VMEM((N_GROUPS, TM_IN, GROUP_DIM), F32)],
        compiler_params=pltpu.CompilerParams(
            dimension_semantics=("arbitrary",), vmem_limit_bytes=VMEM_LIMIT_BYTES),
        name="in_proj_gmlp",
    )(x2, norm_g, w_in, gv, ws, bs, w_up, w_down, w_out, gmlp)


def _radix_tables():
    q = SEQ // RADIX
    kp = np.arange(q, dtype=np.int64)[:, None]
    m = np.arange(q, dtype=np.int64)[None, :]
    out = np.empty((RADIX, 2, q, q), np.float32)
    for r in range(RADIX):
        ang = 2.0 * np.pi * ((kp * (RADIX * m + r)) % SEQ).astype(np.float64) / SEQ
        out[r, 0] = np.cos(ang)
        out[r, 1] = np.sin(ang)
    return out


def _fourier_kernel(t_ref, a_ref, m_ref, yf_ref):
    q = SEQ // RADIX
    pc = [_dot(t_ref[r, 0], a_ref[0, r]) for r in range(RADIX)]
    ps = [_dot(t_ref[r, 1], a_ref[0, r]) for r in range(RADIX)]
    ac, bc, cc, dc = pc[0] + pc[2], pc[0] - pc[2], pc[1] + pc[3], pc[1] - pc[3]
    as_, bs, cs, ds = ps[0] + ps[2], ps[0] - ps[2], ps[1] + ps[3], ps[1] - ps[3]
    out_c = (ac + cc, bc - ds, ac - cc, bc + ds)
    out_s = (as_ + cs, bs + dc, as_ - cs, bs - dc)
    for j in range(RADIX):
        c = out_c[j].astype(BF16)
        s = out_s[j].astype(BF16)
        for g in range(m_ref.shape[0]):
            cols = slice(g * GROUP_DIM, (g + 1) * GROUP_DIM)
            y = _dot(jnp.concatenate([c[:, cols], s[:, cols]], axis=1), m_ref[g])
            yf_ref[j * q:(j + 1) * q, cols] = y.astype(BF16)


def _fourier(a, m, batch):
    assert RADIX == 4
    q = SEQ // RADIX
    tables = jnp.asarray(_radix_tables()).astype(BF16)
    groups = TC_FOURIER // GROUP_DIM
    return pl.pallas_call(
        _fourier_kernel,
        grid=(batch, FOURIER_WIDTH // TC_FOURIER),
        in_specs=[
            pl.BlockSpec((RADIX, 2, q, q), lambda b, c: (0, 0, 0, 0), pipeline_mode=pl.Buffered(1)),
            pl.BlockSpec((1, RADIX, q, TC_FOURIER), lambda b, c: (b, 0, 0, c)),
            pl.BlockSpec((groups, 2 * GROUP_DIM, GROUP_DIM), lambda b, c: (c, 0, 0)),
        ],
        out_specs=pl.BlockSpec((SEQ, TC_FOURIER), lambda b, c: (b, c)),
        out_shape=jax.ShapeDtypeStruct((batch * SEQ, FOURIER_WIDTH), BF16),
        compiler_params=pltpu.CompilerParams(
            dimension_semantics=("arbitrary", "arbitrary"), vmem_limit_bytes=VMEM_LIMIT_BYTES),
        name="fourier_mix",
    )(tables, a, m)


def _out_proj_kernel(yf_ref, yg_ref, x_ref, w_ref, h_ref, hb_ref):
    for rb in range(TM_OUT // SUB_OUT):
        rows = slice(rb * SUB_OUT, (rb + 1) * SUB_OUT)
        mix = jnp.concatenate([yf_ref[rows, :], yg_ref[rows, :]], axis=1)
        h = x_ref[rows, :] + _dot(mix, w_ref[...])
        h_ref[rows, :] = h
        hb_ref[rows, :] = h.astype(BF16)


def _out_proj(yf, yg, x2, w_out):
    rows = x2.shape[0]
    return pl.pallas_call(
        _out_proj_kernel,
        grid=(rows // TM_OUT,),
        in_specs=[
            pl.BlockSpec((TM_OUT, FOURIER_WIDTH), lambda i: (i, 0)),
            pl.BlockSpec((TM_OUT, GMLP_WIDTH), lambda i: (i, 0)),
            pl.BlockSpec((TM_OUT, D_MODEL), lambda i: (i, 0)),
            pl.BlockSpec((D_MODEL, D_MODEL), lambda i: (0, 0), pipeline_mode=pl.Buffered(1)),
        ],
        out_specs=[
            pl.BlockSpec((TM_OUT, D_MODEL), lambda i: (i, 0)),
            pl.BlockSpec((TM_OUT, D_MODEL), lambda i: (i, 0)),
        ],
        out_shape=[
            jax.ShapeDtypeStruct((rows, D_MODEL), F32),
            jax.ShapeDtypeStruct((rows, D_MODEL), BF16),
        ],
        compiler_params=pltpu.CompilerParams(
            dimension_semantics=("arbitrary",), vmem_limit_bytes=58 * 1024 * 1024),
        name="out_proj",
    )(yf, yg, x2, w_out)


def _mlp_kernel(hb_ref, h_hbm, wu_ref, wd_ref, g_ref, o_ref, h_buf, h_sem):
    i = pl.program_id(0)
    f = pl.program_id(1)
    last = pl.num_programs(1) - 1

    def h_copy():
        return pltpu.make_async_copy(h_hbm.at[pl.ds(i * TM_MLP, TM_MLP), :], h_buf, h_sem)

    def mlp_rows(rb):
        rows = slice(rb * SUB_MLP, (rb + 1) * SUB_MLP)
        r = jnp.maximum(_dot(hb_ref[rows, :], wu_ref[0]), 0.0)
        return rows, _dot((r * r).astype(BF16), wd_ref[...])

    @pl.when(f == 0)
    def _():
        h_copy().start()
        for rb in range(TM_MLP // SUB_MLP):
            rows, d = mlp_rows(rb)
            o_ref[rows, :] = d

    @pl.when(jnp.logical_and(f > 0, f < last))
    def _():
        for rb in range(TM_MLP // SUB_MLP):
            rows, d = mlp_rows(rb)
            o_ref[rows, :] += d

    @pl.when(f == last)
    def _():
        h_copy().wait()
        for rb in range(TM_MLP // SUB_MLP):
            rows, d = mlp_rows(rb)
            h = h_buf[rows, :]
            s = _rms_scale(h)
            t = h + (o_ref[rows, :] + d) * (s * s)
            o_ref[rows, :] = t * _rms_scale(t) * g_ref[...]


def _mlp(hb, h, w_up, w_down, norm_g):
    rows = hb.shape[0]
    assert D_FF // TF_MLP >= 2
    return pl.pallas_call(
        _mlp_kernel,
        grid=(rows // TM_MLP, D_FF // TF_MLP),
        in_specs=[
            pl.BlockSpec((TM_MLP, D_MODEL), lambda i, f: (i, 0)),
            pl.BlockSpec(memory_space=pl.ANY),
            pl.BlockSpec((1, D_MODEL, TF_MLP), lambda i, f: (f, 0, 0)),
            pl.BlockSpec((TF_MLP, D_MODEL), lambda i, f: (f, 0)),
            pl.BlockSpec((1, D_MODEL), lambda i, f: (0, 0), pipeline_mode=pl.Buffered(1)),
        ],
        out_specs=pl.BlockSpec((TM_MLP, D_MODEL), lambda i, f: (i, 0)),
        out_shape=jax.ShapeDtypeStruct((rows, D_MODEL), F32),
        scratch_shapes=[pltpu.VMEM((TM_MLP, D_MODEL), F32), pltpu.SemaphoreType.DMA(())],
        compiler_params=pltpu.CompilerParams(
            dimension_semantics=("arbitrary", "arbitrary"), vmem_limit_bytes=VMEM_LIMIT_BYTES),
        name="mlp",
    )(hb, h, w_up, w_down, norm_g)


def kernel(x, norm_mix_g, w_in, fourier_w, gmlp_v_g, gmlp_ws, gmlp_b, w_out,
           norm_mlp_g, w_up, w_down, norm_final_g):
    batch, seq, d = x.shape
    assert (seq, d) == (SEQ, D_MODEL)
    x2 = x.reshape(batch * seq, d)
    bs = jnp.broadcast_to(gmlp_b[:, :, None], (N_HEADS, CHUNK, GROUP_DIM))

    m = _fold_fourier_weights(fourier_w)
    a, yg, w_up_b, w_down_b, w_out_b = _in_proj(
        x2, norm_mix_g.reshape(1, d), w_in.astype(BF16), gmlp_v_g, gmlp_ws.astype(BF16), bs,
        w_up, w_down, w_out, jnp.broadcast_to(norm_mlp_g[:, None], (d, GROUP_DIM)))
    yf = _fourier(a, m, batch)
    h, hb = _out_proj(yf, yg, x2, w_out_b)
    y = _mlp(hb, h, w_up_b, w_down_b, norm_final_g.reshape(1, d))
    return y.reshape(batch, seq, d)
```

```python
import numpy as np
import jax
import jax.numpy as jnp
from jax import lax
from jax.experimental import pallas as pl
from jax.experimental.pallas import tpu as pltpu

D_MODEL = 2048
SEQ = 2048
GROUP_DIM = 128
FOURIER_WIDTH = 1024
GMLP_WIDTH = 1024
N_GROUPS = FOURIER_WIDTH // GROUP_DIM
N_HEADS = GMLP_WIDTH // GROUP_DIM
IN_PROJ_WIDTH = FOURIER_WIDTH + 2 * GMLP_WIDTH
CHUNK = 128
D_FF = 4 * D_MODEL
EPS = 1e-6

F32 = jnp.float32
BF16 = jnp.bfloat16

VMEM_LIMIT_BYTES = 56 * 1024 * 1024

TM_IN = 512
STAGE_COLS = 512
HEADS_PER_DOT = 4
RADIX = 4
TC_FOURIER = 1024
TM_OUT = 1024
SUB_OUT = 512
TM_MLP = 1024
TF_MLP = 1024
SUB_MLP = 512


def _dft_tables(n):
    k = np.arange(n, dtype=np.int64)
    ang = 2.0 * np.pi * ((k[:, None] * k[None, :]) % n).astype(np.float64) / n
    return np.cos(ang).astype(np.float32), np.sin(ang).astype(np.float32)


def _dot(a, b):
    return jnp.dot(a, b, preferred_element_type=F32)


def _rms_scale(v):
    return lax.rsqrt(jnp.mean(v * v, axis=-1, keepdims=True) + EPS)


def _fold_kernel(cc_ref, sc_ref, wf_ref, m_ref):
    scale = 1.0 / np.sqrt(float(SEQ * GROUP_DIM))
    for g in range(N_GROUPS):
        w = wf_ref[g]
        m1 = jnp.dot(cc_ref[...], w, preferred_element_type=F32, precision=lax.Precision.HIGHEST)
        m2 = jnp.dot(sc_ref[...], w, preferred_element_type=F32, precision=lax.Precision.HIGHEST)
        m_ref[g, :GROUP_DIM, :] = (m1 * scale).astype(BF16)
        m_ref[g, GROUP_DIM:, :] = (m2 * (-scale)).astype(BF16)


def _fold_fourier_weights(fourier_w):
    cc, sc = _dft_tables(GROUP_DIM)
    return pl.pallas_call(
        _fold_kernel,
        out_shape=jax.ShapeDtypeStruct((N_GROUPS, 2 * GROUP_DIM, GROUP_DIM), BF16),
        name="fold_fourier_w",
    )(jnp.asarray(cc), jnp.asarray(sc), fourier_w)


def _gelu(z):
    return 0.5 * z * (1.0 + lax.erf(z * np.float32(np.sqrt(0.5))))


def _in_proj_kernel(x_ref, g_ref, w_hbm, gv_ref, ws_ref, bs_ref, wu_ref, wd_ref, wo_ref, gmlp_ref,
                    a_ref, yg_ref, wu_o_ref, wd_o_ref, wo_o_ref, pf_ref, w_ref, stage_ref, w_sem):
    n_stage = IN_PROJ_WIDTH // STAGE_COLS

    def stage_copy(k):
        return pltpu.make_async_copy(w_hbm.at[:, pl.ds(k * STAGE_COLS, STAGE_COLS)],
                                     stage_ref.at[k % 2], w_sem.at[k % 2])

    @pl.when(pl.program_id(0) == 0)
    def _():
        stage_copy(0).start()
        for k in range(n_stage):
            if k + 1 < n_stage:
                stage_copy(k + 1).start()
            stage_copy(k).wait()
            w_ref[:, k * STAGE_COLS:(k + 1) * STAGE_COLS] = stage_ref[k % 2].astype(BF16)

    x = x_ref[...]
    xn = (x * _rms_scale(x) * g_ref[...]).astype(BF16)
    n_chunks = x.shape[0] // CHUNK
    width = HEADS_PER_DOT * GROUP_DIM
    n_blocks = N_HEADS // HEADS_PER_DOT

    def proj_u(hb):
        cu = FOURIER_WIDTH + hb * width
        return _gelu(_dot(xn, w_ref[:, cu:cu + width]))

    def proj_v(hb):
        cv = FOURIER_WIDTH + GMLP_WIDTH + hb * width
        return _gelu(_dot(xn, w_ref[:, cv:cv + width]))

    def gate(hb, u, v):
        for j in range(HEADS_PER_DOT):
            h = hb * HEADS_PER_DOT + j
            cols = slice(j * GROUP_DIM, (j + 1) * GROUP_DIM)
            vh = v[:, cols]
            vn = (vh * _rms_scale(vh) * gv_ref[h:h + 1, :]).astype(BF16)
            vcat = jnp.concatenate(
                [vn[c * CHUNK:(c + 1) * CHUNK, :] for c in range(n_chunks)], axis=1)
            s = _dot(ws_ref[h], vcat)
            for c in range(n_chunks):
                rows = slice(c * CHUNK, (c + 1) * CHUNK)
                sc = s[:, c * GROUP_DIM:(c + 1) * GROUP_DIM] + bs_ref[h]
                yg_ref[rows, h * GROUP_DIM:(h + 1) * GROUP_DIM] = (u[rows, cols] * sc).astype(BF16)

    def fourier_cols(part, n_parts):
        groups = N_GROUPS // n_parts
        c0 = part * groups * GROUP_DIM
        pf = _dot(xn, w_ref[:, c0:c0 + groups * GROUP_DIM])
        for gl in range(groups):
            pf_ref[part * groups + gl] = pf[:, gl * GROUP_DIM:(gl + 1) * GROUP_DIM]
        for r in range(RADIX):
            for gl in range(groups):
                g = part * groups + gl
                a_ref[0, r, :, g * GROUP_DIM:(g + 1) * GROUP_DIM] = (
                    pf_ref[g, pl.ds(r, x.shape[0] // RADIX, stride=RADIX), :].astype(BF16))

    assert n_blocks == 2
    u0, v0 = proj_u(0), proj_v(0)
    u1 = proj_u(1)
    gate(0, u0, v0)
    v1 = proj_v(1)
    for c in range(wu_ref.shape[1] // GROUP_DIM):
        cols = slice(c * GROUP_DIM, (c + 1) * GROUP_DIM)
        wu_o_ref[0, :, cols] = (wu_ref[:, cols] * gmlp_ref[...]).astype(BF16)
    wd_o_ref[...] = wd_ref[...].astype(BF16)
    wo_o_ref[...] = wo_ref[...].astype(BF16)
    fourier_cols(0, 2)
    gate(1, u1, v1)
    fourier_cols(1, 2)


def _in_proj(x2, norm_g, w_in, gv, ws, bs, w_up, w_down, w_out, gmlp):
    rows = x2.shape[0]
    steps = rows // TM_IN
    ff_cols = D_FF // steps
    out_rows = D_MODEL // steps
    per_tile = TF_MLP // ff_cols
    assert ff_cols * steps == D_FF and out_rows * steps == D_MODEL and per_tile * ff_cols == TF_MLP
    tiles_per_seq = SEQ // TM_IN
    res_rows = TM_IN // RADIX
    const = lambda *shape: pl.BlockSpec(shape, lambda i: (0,) * len(shape),
                                        pipeline_mode=pl.Buffered(1))
    return pl.pallas_call(
        _in_proj_kernel,
        grid=(steps,),
        in_specs=[
            pl.BlockSpec((TM_IN, D_MODEL), lambda i: (i, 0)),
            const(1, D_MODEL),
            pl.BlockSpec(memory_space=pl.ANY),
            const(N_HEADS, GROUP_DIM),
            const(N_HEADS, CHUNK, CHUNK),
            const(N_HEADS, CHUNK, GROUP_DIM),
            pl.BlockSpec((D_MODEL, ff_cols), lambda i: (0, i)),
            pl.BlockSpec((ff_cols, D_MODEL), lambda i: (i, 0)),
            pl.BlockSpec((out_rows, D_MODEL), lambda i: (i, 0)),
            const(D_MODEL, GROUP_DIM),
        ],
        out_specs=[
            pl.BlockSpec((1, RADIX, res_rows, FOURIER_WIDTH),
                         lambda i: (i // tiles_per_seq, 0, i % tiles_per_seq, 0)),
            pl.BlockSpec((TM_IN, GMLP_WIDTH), lambda i: (i, 0)),
            pl.BlockSpec((1, D_MODEL, ff_cols), lambda i: (i // per_tile, 0, i % per_tile)),
            pl.BlockSpec((ff_cols, D_MODEL), lambda i: (i, 0)),
            pl.BlockSpec((out_rows, D_MODEL), lambda i: (i, 0)),
        ],
        out_shape=[
            jax.ShapeDtypeStruct((rows // SEQ, RADIX, SEQ // RADIX, FOURIER_WIDTH), BF16),
            jax.ShapeDtypeStruct((rows, GMLP_WIDTH), BF16),
            jax.ShapeDtypeStruct((D_FF // TF_MLP, D_MODEL, TF_MLP), BF16),
            jax.ShapeDtypeStruct((D_FF, D_MODEL), BF16),
            jax.ShapeDtypeStruct((D_MODEL, D_MODEL), BF16),
        ],
        scratch_shapes=[pltpu.VMEM((N_GROUPS, TM_IN, GROUP_DIM), F32),
                        pltpu.VMEM((D_MODEL, IN_PROJ_WIDTH), BF16),
                        pltpu.VMEM((2, D_MODEL, STAGE_COLS), F32),
                        pltpu.SemaphoreType.DMA((2,))],
        compiler_params=pltpu.CompilerParams(
            dimension_semantics=("arbitrary",), vmem_limit_bytes=VMEM_LIMIT_BYTES),
        name="in_proj_gmlp",
    )(x2, norm_g, w_in, gv, ws, bs, w_up, w_down, w_out, gmlp)


def _radix_tables():
    q = SEQ // RADIX
    kp = np.arange(q, dtype=np.int64)[:, None]
    m = np.arange(q, dtype=np.int64)[None, :]
    out = np.empty((RADIX, 2, q, q), np.float32)
    for r in range(RADIX):
        ang = 2.0 * np.pi * ((kp * (RADIX * m + r)) % SEQ).astype(np.float64) / SEQ
        out[r, 0] = np.cos(ang)
        out[r, 1] = np.sin(ang)
    return out


def _fourier_kernel(t_ref, a_ref, m_ref, yf_ref):
    q = SEQ // RADIX
    pc = [_dot(t_ref[r, 0], a_ref[0, r]) for r in range(RADIX)]
    ps = [_dot(t_ref[r, 1], a_ref[0, r]) for r in range(RADIX)]
    ac, bc, cc, dc = pc[0] + pc[2], pc[0] - pc[2], pc[1] + pc[3], pc[1] - pc[3]
    as_, bs, cs, ds = ps[0] + ps[2], ps[0] - ps[2], ps[1] + ps[3], ps[1] - ps[3]
    out_c = (ac + cc, bc - ds, ac - cc, bc + ds)
    out_s = (as_ + cs, bs + dc, as_ - cs, bs - dc)
    for j in range(RADIX):
        c = out_c[j].astype(BF16)
        s = out_s[j].astype(BF16)
        for g in range(m_ref.shape[0]):
            cols = slice(g * GROUP_DIM, (g + 1) * GROUP_DIM)
            y = _dot(jnp.concatenate([c[:, cols], s[:, cols]], axis=1), m_ref[g])
            yf_ref[j * q:(j + 1) * q, cols] = y.astype(BF16)


def _fourier(a, m, batch):
    assert RADIX == 4
    q = SEQ // RADIX
    tables = jnp.asarray(_radix_tables()).astype(BF16)
    groups = TC_FOURIER // GROUP_DIM
    return pl.pallas_call(
        _fourier_kernel,
        grid=(batch, FOURIER_WIDTH // TC_FOURIER),
        in_specs=[
            pl.BlockSpec((RADIX, 2, q, q), lambda b, c: (0, 0, 0, 0), pipeline_mode=pl.Buffered(1)),
            pl.BlockSpec((1, RADIX, q, TC_FOURIER), lambda b, c: (b, 0, 0, c)),
            pl.BlockSpec((groups, 2 * GROUP_DIM, GROUP_DIM), lambda b, c: (c, 0, 0)),
        ],
        out_specs=pl.BlockSpec((SEQ, TC_FOURIER), lambda b, c: (b, c)),
        out_shape=jax.ShapeDtypeStruct((batch * SEQ, FOURIER_WIDTH), BF16),
        compiler_params=pltpu.CompilerParams(
            dimension_semantics=("arbitrary", "arbitrary"), vmem_limit_bytes=VMEM_LIMIT_BYTES),
        name="fourier_mix",
    )(tables, a, m)


def _out_proj_kernel(yf_ref, yg_ref, x_ref, w_ref, h_ref, hb_ref):
    for rb in range(TM_OUT // SUB_OUT):
        rows = slice(rb * SUB_OUT, (rb + 1) * SUB_OUT)
        mix = jnp.concatenate([yf_ref[rows, :], yg_ref[rows, :]], axis=1)
        h = x_ref[rows, :] + _dot(mix, w_ref[...])
        h_ref[rows, :] = h
        hb_ref[rows, :] = h.astype(BF16)


def _out_proj(yf, yg, x2, w_out):
    rows = x2.shape[0]
    return pl.pallas_call(
        _out_proj_kernel,
        grid=(rows // TM_OUT,),
        in_specs=[
            pl.BlockSpec((TM_OUT, FOURIER_WIDTH), lambda i: (i, 0)),
            pl.BlockSpec((TM_OUT, GMLP_WIDTH), lambda i: (i, 0)),
            pl.BlockSpec((TM_OUT, D_MODEL), lambda i: (i, 0)),
            pl.BlockSpec((D_MODEL, D_MODEL), lambda i: (0, 0), pipeline_mode=pl.Buffered(1)),
        ],
        out_specs=[
            pl.BlockSpec((TM_OUT, D_MODEL), lambda i: (i, 0)),
            pl.BlockSpec((TM_OUT, D_MODEL), lambda i: (i, 0)),
        ],
        out_shape=[
            jax.ShapeDtypeStruct((rows, D_MODEL), F32),
            jax.ShapeDtypeStruct((rows, D_MODEL), BF16),
        ],
        compiler_params=pltpu.CompilerParams(
            dimension_semantics=("arbitrary",), vmem_limit_bytes=58 * 1024 * 1024),
        name="out_proj",
    )(yf, yg, x2, w_out)


def _mlp_kernel(hb_ref, h_hbm, wu_ref, wd_ref, g_ref, o_ref, h_buf, h_sem):
    i = pl.program_id(0)
    f = pl.program_id(1)
    last = pl.num_programs(1) - 1

    def h_copy():
        return pltpu.make_async_copy(h_hbm.at[pl.ds(i * TM_MLP, TM_MLP), :], h_buf, h_sem)

    def mlp_rows(rb):
        rows = slice(rb * SUB_MLP, (rb + 1) * SUB_MLP)
        r = jnp.maximum(_dot(hb_ref[rows, :], wu_ref[0]), 0.0)
        return rows, _dot((r * r).astype(BF16), wd_ref[...])

    @pl.when(f == 0)
    def _():
        h_copy().start()
        for rb in range(TM_MLP // SUB_MLP):
            rows, d = mlp_rows(rb)
            o_ref[rows, :] = d

    @pl.when(jnp.logical_and(f > 0, f < last))
    def _():
        for rb in range(TM_MLP // SUB_MLP):
            rows, d = mlp_rows(rb)
            o_ref[rows, :] += d

    @pl.when(f == last)
    def _():
        h_copy().wait()
        for rb in range(TM_MLP // SUB_MLP):
            rows, d = mlp_rows(rb)
            h = h_buf[rows, :]
            s = _rms_scale(h)
            t = h + (o_ref[rows, :] + d) * (s * s)
            o_ref[rows, :] = t * _rms_scale(t) * g_ref[...]


def _mlp(hb, h, w_up, w_down, norm_g):
    rows = hb.shape[0]
    assert D_FF // TF_MLP >= 2
    return pl.pallas_call(
        _mlp_kernel,
        grid=(rows // TM_MLP, D_FF // TF_MLP),
        in_specs=[
            pl.BlockSpec((TM_MLP, D_MODEL), lambda i, f: (i, 0)),
            pl.BlockSpec(memory_space=pl.ANY),
            pl.BlockSpec((1, D_MODEL, TF_MLP), lambda i, f: (f, 0, 0)),
            pl.BlockSpec((TF_MLP, D_MODEL), lambda i, f: (f, 0)),
            pl.BlockSpec((1, D_MODEL), lambda i, f: (0, 0), pipeline_mode=pl.Buffered(1)),
        ],
        out_specs=pl.BlockSpec((TM_MLP, D_MODEL), lambda i, f: (i, 0)),
        out_shape=jax.ShapeDtypeStruct((rows, D_MODEL), F32),
        scratch_shapes=[pltpu.VMEM((TM_MLP, D_MODEL), F32), pltpu.SemaphoreType.DMA(())],
        compiler_params=pltpu.CompilerParams(
            dimension_semantics=("arbitrary", "arbitrary"), vmem_limit_bytes=VMEM_LIMIT_BYTES),
        name="mlp",
    )(hb, h, w_up, w_down, norm_g)


def kernel(x, norm_mix_g, w_in, fourier_w, gmlp_v_g, gmlp_ws, gmlp_b, w_out,
           norm_mlp_g, w_up, w_down, norm_final_g):
    batch, seq, d = x.shape
    assert (seq, d) == (SEQ, D_MODEL)
    x2 = x.reshape(batch * seq, d)
    bs = jnp.broadcast_to(gmlp_b[:, :, None], (N_HEADS, CHUNK, GROUP_DIM))

    m = _fold_fourier_weights(fourier_w)
    a, yg, w_up_b, w_down_b, w_out_b = _in_proj(
        x2, norm_mix_g.reshape(1, d), w_in, gmlp_v_g, gmlp_ws.astype(BF16), bs,
        w_up, w_down, w_out, jnp.broadcast_to(norm_mlp_g[:, None], (d, GROUP_DIM)))
    yf = _fourier(a, m, batch)
    h, hb = _out_proj(yf, yg, x2, w_out_b)
    y = _mlp(hb, h, w_up_b, w_down_b, norm_final_g.reshape(1, d))
    return y.reshape(batch, seq, d)
```

```python
import numpy as np
import jax
import jax.numpy as jnp
from jax import lax
from jax.experimental import pallas as pl
from jax.experimental.pallas import tpu as pltpu

D_MODEL = 2048
SEQ = 2048
GROUP_DIM = 128
FOURIER_WIDTH = 1024
GMLP_WIDTH = 1024
N_GROUPS = FOURIER_WIDTH // GROUP_DIM
N_HEADS = GMLP_WIDTH // GROUP_DIM
IN_PROJ_WIDTH = FOURIER_WIDTH + 2 * GMLP_WIDTH
CHUNK = 128
D_FF = 4 * D_MODEL
EPS = 1e-6

F32 = jnp.float32
BF16 = jnp.bfloat16

VMEM_LIMIT_BYTES = 56 * 1024 * 1024
VMEM_LIMIT_OUT_PROJ_BYTES = 58 * 1024 * 1024

TM_IN = 512
STAGE_COLS = 512
HEADS_PER_DOT = 4
RADIX = 4
TC_FOURIER = 1024
TM_OUT = 1024
SUB_OUT = 512
TM_MLP = 1024
TF_MLP = 1024
SUB_MLP = 512


def _dft_tables(n):
    k = np.arange(n, dtype=np.int64)
    ang = 2.0 * np.pi * ((k[:, None] * k[None, :]) % n).astype(np.float64) / n
    return np.cos(ang).astype(np.float32), np.sin(ang).astype(np.float32)


def _dot(a, b):
    return jnp.dot(a, b, preferred_element_type=F32)


def _rms_scale(v):
    return lax.rsqrt(jnp.mean(v * v, axis=-1, keepdims=True) + EPS)


def _fold_kernel(cc_ref, sc_ref, wf_ref, m_ref):
    scale = 1.0 / np.sqrt(float(SEQ * GROUP_DIM))
    for g in range(N_GROUPS):
        w = wf_ref[g]
        m1 = jnp.dot(cc_ref[...], w, preferred_element_type=F32, precision=lax.Precision.HIGHEST)
        m2 = jnp.dot(sc_ref[...], w, preferred_element_type=F32, precision=lax.Precision.HIGHEST)
        m_ref[g, :GROUP_DIM, :] = (m1 * scale).astype(BF16)
        m_ref[g, GROUP_DIM:, :] = (m2 * (-scale)).astype(BF16)


def _fold_fourier_weights(fourier_w):
    cc, sc = _dft_tables(GROUP_DIM)
    return pl.pallas_call(
        _fold_kernel,
        out_shape=jax.ShapeDtypeStruct((N_GROUPS, 2 * GROUP_DIM, GROUP_DIM), BF16),
        name="fold_fourier_w",
    )(jnp.asarray(cc), jnp.asarray(sc), fourier_w)


def _gelu(z):
    return 0.5 * z * (1.0 + lax.erf(z * np.float32(np.sqrt(0.5))))


def _in_proj_kernel(x_ref, g_ref, w_hbm, gv_ref, ws_ref, bs_ref, wu_ref, wd_ref, wo_ref, gmlp_ref,
                    a_ref, yg_ref, wu_o_ref, wd_o_ref, wo_o_ref, pf_ref, w_ref, stage_ref, w_sem):
    n_stage = IN_PROJ_WIDTH // STAGE_COLS

    def stage_copy(k):
        return pltpu.make_async_copy(w_hbm.at[:, pl.ds(k * STAGE_COLS, STAGE_COLS)],
                                     stage_ref.at[k % 2], w_sem.at[k % 2])

    @pl.when(pl.program_id(0) == 0)
    def _():
        stage_copy(0).start()
        for k in range(n_stage):
            if k + 1 < n_stage:
                stage_copy(k + 1).start()
            stage_copy(k).wait()
            w_ref[:, k * STAGE_COLS:(k + 1) * STAGE_COLS] = stage_ref[k % 2].astype(BF16)

    x = x_ref[...]
    xn = (x * _rms_scale(x) * g_ref[...]).astype(BF16)
    n_chunks = x.shape[0] // CHUNK
    width = HEADS_PER_DOT * GROUP_DIM
    n_blocks = N_HEADS // HEADS_PER_DOT

    def proj_u(hb):
        cu = FOURIER_WIDTH + hb * width
        return _gelu(_dot(xn, w_ref[:, cu:cu + width]))

    def proj_v(hb):
        cv = FOURIER_WIDTH + GMLP_WIDTH + hb * width
        return _gelu(_dot(xn, w_ref[:, cv:cv + width]))

    def gate(hb, u, v):
        for j in range(HEADS_PER_DOT):
            h = hb * HEADS_PER_DOT + j
            cols = slice(j * GROUP_DIM, (j + 1) * GROUP_DIM)
            vh = v[:, cols]
            vn = (vh * _rms_scale(vh) * gv_ref[h:h + 1, :]).astype(BF16)
            vcat = jnp.concatenate(
                [vn[c * CHUNK:(c + 1) * CHUNK, :] for c in range(n_chunks)], axis=1)
            s = _dot(ws_ref[h], vcat)
            for c in range(n_chunks):
                rows = slice(c * CHUNK, (c + 1) * CHUNK)
                sc = s[:, c * GROUP_DIM:(c + 1) * GROUP_DIM] + bs_ref[h]
                yg_ref[rows, h * GROUP_DIM:(h + 1) * GROUP_DIM] = (u[rows, cols] * sc).astype(BF16)

    def fourier_cols(part, n_parts):
        groups = N_GROUPS // n_parts
        c0 = part * groups * GROUP_DIM
        pf = _dot(xn, w_ref[:, c0:c0 + groups * GROUP_DIM])
        for gl in range(groups):
            pf_ref[part * groups + gl] = pf[:, gl * GROUP_DIM:(gl + 1) * GROUP_DIM]
        for r in range(RADIX):
            for gl in range(groups):
                g = part * groups + gl
                a_ref[0, r, :, g * GROUP_DIM:(g + 1) * GROUP_DIM] = (
                    pf_ref[g, pl.ds(r, x.shape[0] // RADIX, stride=RADIX), :].astype(BF16))

    assert n_blocks == 2
    u0, v0 = proj_u(0), proj_v(0)
    u1 = proj_u(1)
    gate(0, u0, v0)
    v1 = proj_v(1)
    for c in range(wu_ref.shape[1] // GROUP_DIM):
        cols = slice(c * GROUP_DIM, (c + 1) * GROUP_DIM)
        wu_o_ref[0, :, cols] = (wu_ref[:, cols] * gmlp_ref[...]).astype(BF16)
    wd_o_ref[...] = wd_ref[...].astype(BF16)
    wo_o_ref[...] = wo_ref[...].astype(BF16)
    fourier_cols(0, 2)
    gate(1, u1, v1)
    fourier_cols(1, 2)


def _in_proj(x2, norm_g, w_in, gv, ws, bs, w_up, w_down, w_out, gmlp):
    rows = x2.shape[0]
    steps = rows // TM_IN
    ff_cols = D_FF // steps
    out_rows = D_MODEL // steps
    per_tile = TF_MLP // ff_cols
    assert ff_cols * steps == D_FF and out_rows * steps == D_MODEL and per_tile * ff_cols == TF_MLP
    tiles_per_seq = SEQ // TM_IN
    res_rows = TM_IN // RADIX
    const = lambda *shape: pl.BlockSpec(shape, lambda i: (0,) * len(shape),
                                        pipeline_mode=pl.Buffered(1))
    return pl.pallas_call(
        _in_proj_kernel,
        grid=(steps,),
        in_specs=[
            pl.BlockSpec((TM_IN, D_MODEL), lambda i: (i, 0)),
            const(1, D_MODEL),
            pl.BlockSpec(memory_space=pl.ANY),
            const(N_HEADS, GROUP_DIM),
            const(N_HEADS, CHUNK, CHUNK),
            const(N_HEADS, CHUNK, GROUP_DIM),
            pl.BlockSpec((D_MODEL, ff_cols), lambda i: (0, i)),
            pl.BlockSpec((ff_cols, D_MODEL), lambda i: (i, 0)),
            pl.BlockSpec((out_rows, D_MODEL), lambda i: (i, 0)),
            const(D_MODEL, GROUP_DIM),
        ],
        out_specs=[
            pl.BlockSpec((1, RADIX, res_rows, FOURIER_WIDTH),
                         lambda i: (i // tiles_per_seq, 0, i % tiles_per_seq, 0)),
            pl.BlockSpec((TM_IN, GMLP_WIDTH), lambda i: (i, 0)),
            pl.BlockSpec((1, D_MODEL, ff_cols), lambda i: (i // per_tile, 0, i % per_tile)),
            pl.BlockSpec((ff_cols, D_MODEL), lambda i: (i, 0)),
            pl.BlockSpec((out_rows, D_MODEL), lambda i: (i, 0)),
        ],
        out_shape=[
            jax.ShapeDtypeStruct((rows // SEQ, RADIX, SEQ // RADIX, FOURIER_WIDTH), BF16),
            jax.ShapeDtypeStruct((rows, GMLP_WIDTH), BF16),
            jax.ShapeDtypeStruct((D_FF // TF_MLP, D_MODEL, TF_MLP), BF16),
            jax.ShapeDtypeStruct((D_FF, D_MODEL), BF16),
            jax.ShapeDtypeStruct((D_MODEL, D_MODEL), BF16),
        ],
        scratch_shapes=[pltpu.VMEM((N_GROUPS, TM_IN, GROUP_DIM), F32),
                        pltpu.VMEM((D_MODEL, IN_PROJ_WIDTH), BF16),
                        pltpu.VMEM((2, D_MODEL, STAGE_COLS), F32),
                        pltpu.SemaphoreType.DMA((2,))],
        compiler_params=pltpu.CompilerParams(
            dimension_semantics=("arbitrary",), vmem_limit_bytes=VMEM_LIMIT_BYTES),
        name="in_proj_gmlp",
    )(x2, norm_g, w_in, gv, ws, bs, w_up, w_down, w_out, gmlp)


def _radix_tables():
    q = SEQ // RADIX
    kp = np.arange(q, dtype=np.int64)[:, None]
    m = np.arange(q, dtype=np.int64)[None, :]
    out = np.empty((RADIX, 2, q, q), np.float32)
    for r in range(RADIX):
        ang = 2.0 * np.pi * ((kp * (RADIX * m + r)) % SEQ).astype(np.float64) / SEQ
        out[r, 0] = np.cos(ang)
        out[r, 1] = np.sin(ang)
    return out


def _fourier_kernel(t_ref, a_ref, m_ref, yf_ref):
    q = SEQ // RADIX
    pc = [_dot(t_ref[r, 0], a_ref[0, r]) for r in range(RADIX)]
    ps = [_dot(t_ref[r, 1], a_ref[0, r]) for r in range(RADIX)]
    ac, bc, cc, dc = pc[0] + pc[2], pc[0] - pc[2], pc[1] + pc[3], pc[1] - pc[3]
    as_, bs, cs, ds = ps[0] + ps[2], ps[0] - ps[2], ps[1] + ps[3], ps[1] - ps[3]
    out_c = (ac + cc, bc - ds, ac - cc, bc + ds)
    out_s = (as_ + cs, bs + dc, as_ - cs, bs - dc)
    for j in range(RADIX):
        c = out_c[j].astype(BF16)
        s = out_s[j].astype(BF16)
        for g in range(m_ref.shape[0]):
            cols = slice(g * GROUP_DIM, (g + 1) * GROUP_DIM)
            y = _dot(jnp.concatenate([c[:, cols], s[:, cols]], axis=1), m_ref[g])
            yf_ref[j * q:(j + 1) * q, cols] = y.astype(BF16)


def _fourier(a, m, batch):
    assert RADIX == 4
    q = SEQ // RADIX
    tables = jnp.asarray(_radix_tables()).astype(BF16)
    groups = TC_FOURIER // GROUP_DIM
    return pl.pallas_call(
        _fourier_kernel,
        grid=(batch, FOURIER_WIDTH // TC_FOURIER),
        in_specs=[
            pl.BlockSpec((RADIX, 2, q, q), lambda b, c: (0, 0, 0, 0), pipeline_mode=pl.Buffered(1)),
            pl.BlockSpec((1, RADIX, q, TC_FOURIER), lambda b, c: (b, 0, 0, c)),
            pl.BlockSpec((groups, 2 * GROUP_DIM, GROUP_DIM), lambda b, c: (c, 0, 0)),
        ],
        out_specs=pl.BlockSpec((SEQ, TC_FOURIER), lambda b, c: (b, c)),
        out_shape=jax.ShapeDtypeStruct((batch * SEQ, FOURIER_WIDTH), BF16),
        compiler_params=pltpu.CompilerParams(
            dimension_semantics=("arbitrary", "arbitrary"), vmem_limit_bytes=VMEM_LIMIT_BYTES),
        name="fourier_mix",
    )(tables, a, m)


def _out_proj_kernel(yf_ref, yg_ref, x_ref, w_ref, h_ref, hb_ref):
    for rb in range(TM_OUT // SUB_OUT):
        rows = slice(rb * SUB_OUT, (rb + 1) * SUB_OUT)
        mix = jnp.concatenate([yf_ref[rows, :], yg_ref[rows, :]], axis=1)
        h = x_ref[rows, :] + _dot(mix, w_ref[...])
        h_ref[rows, :] = h
        hb_ref[rows, :] = h.astype(BF16)


def _out_proj(yf, yg, x2, w_out):
    rows = x2.shape[0]
    return pl.pallas_call(
        _out_proj_kernel,
        grid=(rows // TM_OUT,),
        in_specs=[
            pl.BlockSpec((TM_OUT, FOURIER_WIDTH), lambda i: (i, 0)),
            pl.BlockSpec((TM_OUT, GMLP_WIDTH), lambda i: (i, 0)),
            pl.BlockSpec((TM_OUT, D_MODEL), lambda i: (i, 0)),
            pl.BlockSpec((D_MODEL, D_MODEL), lambda i: (0, 0), pipeline_mode=pl.Buffered(1)),
        ],
        out_specs=[
            pl.BlockSpec((TM_OUT, D_MODEL), lambda i: (i, 0)),
            pl.BlockSpec((TM_OUT, D_MODEL), lambda i: (i, 0)),
        ],
        out_shape=[
            jax.ShapeDtypeStruct((rows, D_MODEL), F32),
            jax.ShapeDtypeStruct((rows, D_MODEL), BF16),
        ],
        compiler_params=pltpu.CompilerParams(
            dimension_semantics=("arbitrary",), vmem_limit_bytes=VMEM_LIMIT_OUT_PROJ_BYTES),
        name="out_proj",
    )(yf, yg, x2, w_out)


def _mlp_kernel(hb_ref, h_hbm, wu_ref, wd_ref, g_ref, o_ref, h_buf, h_sem):
    i = pl.program_id(0)
    f = pl.program_id(1)
    last = pl.num_programs(1) - 1

    def h_copy():
        return pltpu.make_async_copy(h_hbm.at[pl.ds(i * TM_MLP, TM_MLP), :], h_buf, h_sem)

    def mlp_rows(rb):
        rows = slice(rb * SUB_MLP, (rb + 1) * SUB_MLP)
        r = jnp.maximum(_dot(hb_ref[rows, :], wu_ref[0]), 0.0)
        return rows, _dot((r * r).astype(BF16), wd_ref[...])

    @pl.when(f == 0)
    def _():
        for rb in range(TM_MLP // SUB_MLP):
            rows, d = mlp_rows(rb)
            o_ref[rows, :] = d
        h_copy().start()

    @pl.when(jnp.logical_and(f > 0, f < last))
    def _():
        for rb in range(TM_MLP // SUB_MLP):
            rows, d = mlp_rows(rb)
            o_ref[rows, :] += d

    @pl.when(f == last)
    def _():
        h_copy().wait()
        for rb in range(TM_MLP // SUB_MLP):
            rows, d = mlp_rows(rb)
            h = h_buf[rows, :]
            s = _rms_scale(h)
            t = h + (o_ref[rows, :] + d) * (s * s)
            o_ref[rows, :] = t * _rms_scale(t) * g_ref[...]


def _mlp(hb, h, w_up, w_down, norm_g):
    rows = hb.shape[0]
    assert D_FF // TF_MLP >= 2
    return pl.pallas_call(
        _mlp_kernel,
        grid=(rows // TM_MLP, D_FF // TF_MLP),
        in_specs=[
            pl.BlockSpec((TM_MLP, D_MODEL), lambda i, f: (i, 0)),
            pl.BlockSpec(memory_space=pl.ANY),
            pl.BlockSpec((1, D_MODEL, TF_MLP), lambda i, f: (f, 0, 0)),
            pl.BlockSpec((TF_MLP, D_MODEL), lambda i, f: (f, 0)),
            pl.BlockSpec((1, D_MODEL), lambda i, f: (0, 0), pipeline_mode=pl.Buffered(1)),
        ],
        out_specs=pl.BlockSpec((TM_MLP, D_MODEL), lambda i, f: (i, 0)),
        out_shape=jax.ShapeDtypeStruct((rows, D_MODEL), F32),
        scratch_shapes=[pltpu.VMEM((TM_MLP, D_MODEL), F32), pltpu.SemaphoreType.DMA(())],
        compiler_params=pltpu.CompilerParams(
            dimension_semantics=("arbitrary", "arbitrary"), vmem_limit_bytes=VMEM_LIMIT_BYTES),
        name="mlp",
    )(hb, h, w_up, w_down, norm_g)


def kernel(x, norm_mix_g, w_in, fourier_w, gmlp_v_g, gmlp_ws, gmlp_b, w_out,
           norm_mlp_g, w_up, w_down, norm_final_g):
    batch, seq, d = x.shape
    assert (seq, d) == (SEQ, D_MODEL)
    x2 = x.reshape(batch * seq, d)
    bs = jnp.broadcast_to(gmlp_b[:, :, None], (N_HEADS, CHUNK, GROUP_DIM))

    m = _fold_fourier_weights(fourier_w)
    a, yg, w_up_b, w_down_b, w_out_b = _in_proj(
        x2, norm_mix_g.reshape(1, d), w_in, gmlp_v_g, gmlp_ws.astype(BF16), bs,
        w_up, w_down, w_out, jnp.broadcast_to(norm_mlp_g[:, None], (d, GROUP_DIM)))
    yf = _fourier(a, m, batch)
    h, hb = _out_proj(yf, yg, x2, w_out_b)
    y = _mlp(hb, h, w_up_b, w_down_b, norm_final_g.reshape(1, d))
    return y.reshape(batch, seq, d)
```

```python
import numpy as np
import jax
import jax.numpy as jnp
from jax import lax
from jax.experimental import pallas as pl
from jax.experimental.pallas import tpu as pltpu

D_MODEL = 2048
SEQ = 2048
GROUP_DIM = 128
FOURIER_WIDTH = 1024
GMLP_WIDTH = 1024
N_GROUPS = FOURIER_WIDTH // GROUP_DIM
N_HEADS = GMLP_WIDTH // GROUP_DIM
IN_PROJ_WIDTH = FOURIER_WIDTH + 2 * GMLP_WIDTH
CHUNK = 128
D_FF = 4 * D_MODEL
EPS = 1e-6

F32 = jnp.float32
BF16 = jnp.bfloat16

VMEM_LIMIT_BYTES = 56 * 1024 * 1024
VMEM_LIMIT_OUT_PROJ_BYTES = 58 * 1024 * 1024

TM_IN = 512
STAGE_COLS = 512
HEADS_PER_DOT = 4
RADIX = 4
TM_OUT = 1024
SUB_OUT = 512
TM_MLP = 1024
TF_MLP = 1024
SUB_MLP = 512


def _dft_tables(n):
    k = np.arange(n, dtype=np.int64)
    ang = 2.0 * np.pi * ((k[:, None] * k[None, :]) % n).astype(np.float64) / n
    return np.cos(ang).astype(np.float32), np.sin(ang).astype(np.float32)


def _dot(a, b):
    return jnp.dot(a, b, preferred_element_type=F32)


def _rms_scale(v):
    return lax.rsqrt(jnp.mean(v * v, axis=-1, keepdims=True) + EPS)


def _fold_fourier_weights(cc_ref, sc_ref, wf_ref, m_ref):
    scale = 1.0 / np.sqrt(float(SEQ * GROUP_DIM))
    for g in range(N_GROUPS):
        w = wf_ref[g]
        m1 = jnp.dot(cc_ref[...], w, preferred_element_type=F32, precision=lax.Precision.HIGHEST)
        m2 = jnp.dot(sc_ref[...], w, preferred_element_type=F32, precision=lax.Precision.HIGHEST)
        m_ref[g, :GROUP_DIM, :] = (m1 * scale).astype(BF16)
        m_ref[g, GROUP_DIM:, :] = (m2 * (-scale)).astype(BF16)


def _gelu(z):
    return 0.5 * z * (1.0 + lax.erf(z * np.float32(np.sqrt(0.5))))


def _in_proj_kernel(x_ref, g_ref, w_hbm, gv_ref, ws_ref, bs_ref, wu_ref, wd_ref, wo_ref, gmlp_ref,
                    a_ref, yg_ref, wu_o_ref, wd_o_ref, wo_o_ref, pf_ref, w_ref, stage_ref, w_sem):
    n_stage = IN_PROJ_WIDTH // STAGE_COLS

    def stage_copy(k):
        return pltpu.make_async_copy(w_hbm.at[:, pl.ds(k * STAGE_COLS, STAGE_COLS)],
                                     stage_ref.at[k % 2], w_sem.at[k % 2])

    @pl.when(pl.program_id(0) == 0)
    def _():
        stage_copy(0).start()
        for k in range(n_stage):
            if k + 1 < n_stage:
                stage_copy(k + 1).start()
            stage_copy(k).wait()
            w_ref[:, k * STAGE_COLS:(k + 1) * STAGE_COLS] = stage_ref[k % 2].astype(BF16)

    x = x_ref[...]
    xn = (x * _rms_scale(x) * g_ref[...]).astype(BF16)
    n_chunks = x.shape[0] // CHUNK
    width = HEADS_PER_DOT * GROUP_DIM
    n_blocks = N_HEADS // HEADS_PER_DOT

    def proj_u(hb):
        cu = FOURIER_WIDTH + hb * width
        return _gelu(_dot(xn, w_ref[:, cu:cu + width]))

    def proj_v(hb):
        cv = FOURIER_WIDTH + GMLP_WIDTH + hb * width
        return _gelu(_dot(xn, w_ref[:, cv:cv + width]))

    def gate(hb, u, v):
        for j in range(HEADS_PER_DOT):
            h = hb * HEADS_PER_DOT + j
            cols = slice(j * GROUP_DIM, (j + 1) * GROUP_DIM)
            vh = v[:, cols]
            vn = (vh * _rms_scale(vh) * gv_ref[h:h + 1, :]).astype(BF16)
            vcat = jnp.concatenate(
                [vn[c * CHUNK:(c + 1) * CHUNK, :] for c in range(n_chunks)], axis=1)
            s = _dot(ws_ref[h].astype(BF16), vcat)
            for c in range(n_chunks):
                rows = slice(c * CHUNK, (c + 1) * CHUNK)
                sc = s[:, c * GROUP_DIM:(c + 1) * GROUP_DIM] + bs_ref[h]
                yg_ref[rows, h * GROUP_DIM:(h + 1) * GROUP_DIM] = (u[rows, cols] * sc).astype(BF16)

    def fourier_cols(part, n_parts):
        groups = N_GROUPS // n_parts
        c0 = part * groups * GROUP_DIM
        pf = _dot(xn, w_ref[:, c0:c0 + groups * GROUP_DIM])
        for gl in range(groups):
            pf_ref[part * groups + gl] = pf[:, gl * GROUP_DIM:(gl + 1) * GROUP_DIM]
        for r in range(RADIX):
            for gl in range(groups):
                g = part * groups + gl
                a_ref[0, r, :, g * GROUP_DIM:(g + 1) * GROUP_DIM] = (
                    pf_ref[g, pl.ds(r, x.shape[0] // RADIX, stride=RADIX), :].astype(BF16))

    assert n_blocks == 2
    u0, v0 = proj_u(0), proj_v(0)
    u1 = proj_u(1)
    gate(0, u0, v0)
    v1 = proj_v(1)
    for c in range(wu_ref.shape[1] // GROUP_DIM):
        cols = slice(c * GROUP_DIM, (c + 1) * GROUP_DIM)
        wu_o_ref[0, :, cols] = (wu_ref[:, cols] * gmlp_ref[...]).astype(BF16)
    wd_o_ref[...] = wd_ref[...].astype(BF16)
    wo_o_ref[...] = wo_ref[...].astype(BF16)
    fourier_cols(0, 2)
    gate(1, u1, v1)
    fourier_cols(1, 2)


def _in_proj(x2, norm_g, w_in, gv, ws, bs, w_up, w_down, w_out, gmlp):
    rows = x2.shape[0]
    steps = rows // TM_IN
    ff_cols = D_FF // steps
    out_rows = D_MODEL // steps
    per_tile = TF_MLP // ff_cols
    assert ff_cols * steps == D_FF and out_rows * steps == D_MODEL and per_tile * ff_cols == TF_MLP
    tiles_per_seq = SEQ // TM_IN
    res_rows = TM_IN // RADIX
    const = lambda *shape: pl.BlockSpec(shape, lambda i: (0,) * len(shape),
                                        pipeline_mode=pl.Buffered(1))
    return pl.pallas_call(
        _in_proj_kernel,
        grid=(steps,),
        in_specs=[
            pl.BlockSpec((TM_IN, D_MODEL), lambda i: (i, 0)),
            const(1, D_MODEL),
            pl.BlockSpec(memory_space=pl.ANY),
            const(N_HEADS, GROUP_DIM),
            const(N_HEADS, CHUNK, CHUNK),
            const(N_HEADS, CHUNK, GROUP_DIM),
            pl.BlockSpec((D_MODEL, ff_cols), lambda i: (0, i)),
            pl.BlockSpec((ff_cols, D_MODEL), lambda i: (i, 0)),
            pl.BlockSpec((out_rows, D_MODEL), lambda i: (i, 0)),
            const(D_MODEL, GROUP_DIM),
        ],
        out_specs=[
            pl.BlockSpec((1, RADIX, res_rows, FOURIER_WIDTH),
                         lambda i: (i // tiles_per_seq, 0, i % tiles_per_seq, 0)),
            pl.BlockSpec((TM_IN, GMLP_WIDTH), lambda i: (i, 0)),
            pl.BlockSpec((1, D_MODEL, ff_cols), lambda i: (i // per_tile, 0, i % per_tile)),
            pl.BlockSpec((ff_cols, D_MODEL), lambda i: (i, 0)),
            pl.BlockSpec((out_rows, D_MODEL), lambda i: (i, 0)),
        ],
        out_shape=[
            jax.ShapeDtypeStruct((rows // SEQ, RADIX, SEQ // RADIX, FOURIER_WIDTH), BF16),
            jax.ShapeDtypeStruct((rows, GMLP_WIDTH), BF16),
            jax.ShapeDtypeStruct((D_FF // TF_MLP, D_MODEL, TF_MLP), BF16),
            jax.ShapeDtypeStruct((D_FF, D_MODEL), BF16),
            jax.ShapeDtypeStruct((D_MODEL, D_MODEL), BF16),
        ],
        scratch_shapes=[pltpu.VMEM((N_GROUPS, TM_IN, GROUP_DIM), F32),
                        pltpu.VMEM((D_MODEL, IN_PROJ_WIDTH), BF16),
                        pltpu.VMEM((2, D_MODEL, STAGE_COLS), F32),
                        pltpu.SemaphoreType.DMA((2,))],
        compiler_params=pltpu.CompilerParams(
            dimension_semantics=("arbitrary",), vmem_limit_bytes=VMEM_LIMIT_BYTES),
        name="in_proj_gmlp",
    )(x2, norm_g, w_in, gv, ws, bs, w_up, w_down, w_out, gmlp)


def _radix_tables():
    q = SEQ // RADIX
    kp = np.arange(q, dtype=np.int64)[:, None]
    m = np.arange(q, dtype=np.int64)[None, :]
    out = np.empty((RADIX, 2, q, q), np.float32)
    for r in range(RADIX):
        ang = 2.0 * np.pi * ((kp * (RADIX * m + r)) % SEQ).astype(np.float64) / SEQ
        out[r, 0] = np.cos(ang)
        out[r, 1] = np.sin(ang)
    return out


def _fourier_kernel(t_ref, a_ref, cc_ref, sc_ref, wf_ref, yf_ref, m_ref):
    @pl.when(pl.program_id(0) == 0)
    def _():
        _fold_fourier_weights(cc_ref, sc_ref, wf_ref, m_ref)

    q = SEQ // RADIX
    pc = [_dot(t_ref[r, 0], a_ref[0, r]) for r in range(RADIX)]
    ps = [_dot(t_ref[r, 1], a_ref[0, r]) for r in range(RADIX)]
    ac, bc, cc, dc = pc[0] + pc[2], pc[0] - pc[2], pc[1] + pc[3], pc[1] - pc[3]
    as_, bs, cs, ds = ps[0] + ps[2], ps[0] - ps[2], ps[1] + ps[3], ps[1] - ps[3]
    out_c = (ac + cc, bc - ds, ac - cc, bc + ds)
    out_s = (as_ + cs, bs + dc, as_ - cs, bs - dc)
    for j in range(RADIX):
        c = out_c[j].astype(BF16)
        s = out_s[j].astype(BF16)
        for g in range(m_ref.shape[0]):
            cols = slice(g * GROUP_DIM, (g + 1) * GROUP_DIM)
            y = _dot(jnp.concatenate([c[:, cols], s[:, cols]], axis=1), m_ref[g])
            yf_ref[j * q:(j + 1) * q, cols] = y.astype(BF16)


def _fourier(a, fourier_w, batch):
    assert RADIX == 4
    q = SEQ // RADIX
    tables = jnp.asarray(_radix_tables()).astype(BF16)
    cc, sc = _dft_tables(GROUP_DIM)
    const = lambda *shape: pl.BlockSpec(shape, lambda b: (0,) * len(shape),
                                        pipeline_mode=pl.Buffered(1))
    return pl.pallas_call(
        _fourier_kernel,
        grid=(batch,),
        in_specs=[
            const(RADIX, 2, q, q),
            pl.BlockSpec((1, RADIX, q, FOURIER_WIDTH), lambda b: (b, 0, 0, 0)),
            const(GROUP_DIM, GROUP_DIM),
            const(GROUP_DIM, GROUP_DIM),
            const(N_GROUPS, GROUP_DIM, GROUP_DIM),
        ],
        out_specs=pl.BlockSpec((SEQ, FOURIER_WIDTH), lambda b: (b, 0)),
        out_shape=jax.ShapeDtypeStruct((batch * SEQ, FOURIER_WIDTH), BF16),
        scratch_shapes=[pltpu.VMEM((N_GROUPS, 2 * GROUP_DIM, GROUP_DIM), BF16)],
        compiler_params=pltpu.CompilerParams(
            dimension_semantics=("arbitrary",), vmem_limit_bytes=VMEM_LIMIT_BYTES),
        name="fourier_mix",
    )(tables, a, jnp.asarray(cc), jnp.asarray(sc), fourier_w)


def _out_proj_kernel(yf_ref, yg_ref, x_ref, w_ref, h_ref, hb_ref):
    for rb in range(TM_OUT // SUB_OUT):
        rows = slice(rb * SUB_OUT, (rb + 1) * SUB_OUT)
        mix = jnp.concatenate([yf_ref[rows, :], yg_ref[rows, :]], axis=1)
        h = x_ref[rows, :] + _dot(mix, w_ref[...])
        h_ref[rows, :] = h
        hb_ref[rows, :] = h.astype(BF16)


def _out_proj(yf, yg, x2, w_out):
    rows = x2.shape[0]
    return pl.pallas_call(
        _out_proj_kernel,
        grid=(rows // TM_OUT,),
        in_specs=[
            pl.BlockSpec((TM_OUT, FOURIER_WIDTH), lambda i: (i, 0)),
            pl.BlockSpec((TM_OUT, GMLP_WIDTH), lambda i: (i, 0)),
            pl.BlockSpec((TM_OUT, D_MODEL), lambda i: (i, 0)),
            pl.BlockSpec((D_MODEL, D_MODEL), lambda i: (0, 0), pipeline_mode=pl.Buffered(1)),
        ],
        out_specs=[
            pl.BlockSpec((TM_OUT, D_MODEL), lambda i: (i, 0)),
            pl.BlockSpec((TM_OUT, D_MODEL), lambda i: (i, 0)),
        ],
        out_shape=[
            jax.ShapeDtypeStruct((rows, D_MODEL), F32),
            jax.ShapeDtypeStruct((rows, D_MODEL), BF16),
        ],
        compiler_params=pltpu.CompilerParams(
            dimension_semantics=("arbitrary",), vmem_limit_bytes=VMEM_LIMIT_OUT_PROJ_BYTES),
        name="out_proj",
    )(yf, yg, x2, w_out)


def _mlp_kernel(hb_ref, h_hbm, wu_ref, wd_ref, g_ref, o_ref, h_buf, h_sem):
    i = pl.program_id(0)
    f = pl.program_id(1)
    last = pl.num_programs(1) - 1

    def h_copy():
        return pltpu.make_async_copy(h_hbm.at[pl.ds(i * TM_MLP, TM_MLP), :], h_buf, h_sem)

    def mlp_rows(rb):
        rows = slice(rb * SUB_MLP, (rb + 1) * SUB_MLP)
        r = jnp.maximum(_dot(hb_ref[rows, :], wu_ref[0]), 0.0)
        return rows, _dot((r * r).astype(BF16), wd_ref[...])

    @pl.when(f == 0)
    def _():
        for rb in range(TM_MLP // SUB_MLP):
            rows, d = mlp_rows(rb)
            o_ref[rows, :] = d
        h_copy().start()

    @pl.when(jnp.logical_and(f > 0, f < last))
    def _():
        for rb in range(TM_MLP // SUB_MLP):
            rows, d = mlp_rows(rb)
            o_ref[rows, :] += d

    @pl.when(f == last)
    def _():
        h_copy().wait()
        for rb in range(TM_MLP // SUB_MLP):
            rows, d = mlp_rows(rb)
            h = h_buf[rows, :]
            s = _rms_scale(h)
            t = h + (o_ref[rows, :] + d) * (s * s)
            o_ref[rows, :] = t * _rms_scale(t) * g_ref[...]


def _mlp(hb, h, w_up, w_down, norm_g):
    rows = hb.shape[0]
    assert D_FF // TF_MLP >= 2
    return pl.pallas_call(
        _mlp_kernel,
        grid=(rows // TM_MLP, D_FF // TF_MLP),
        in_specs=[
            pl.BlockSpec((TM_MLP, D_MODEL), lambda i, f: (i, 0)),
            pl.BlockSpec(memory_space=pl.ANY),
            pl.BlockSpec((1, D_MODEL, TF_MLP), lambda i, f: (f, 0, 0)),
            pl.BlockSpec((TF_MLP, D_MODEL), lambda i, f: (f, 0)),
            pl.BlockSpec((1, D_MODEL), lambda i, f: (0, 0), pipeline_mode=pl.Buffered(1)),
        ],
        out_specs=pl.BlockSpec((TM_MLP, D_MODEL), lambda i, f: (i, 0)),
        out_shape=jax.ShapeDtypeStruct((rows, D_MODEL), F32),
        scratch_shapes=[pltpu.VMEM((TM_MLP, D_MODEL), F32), pltpu.SemaphoreType.DMA(())],
        compiler_params=pltpu.CompilerParams(
            dimension_semantics=("arbitrary", "arbitrary"), vmem_limit_bytes=VMEM_LIMIT_BYTES),
        name="mlp",
    )(hb, h, w_up, w_down, norm_g)


def kernel(x, norm_mix_g, w_in, fourier_w, gmlp_v_g, gmlp_ws, gmlp_b, w_out,
           norm_mlp_g, w_up, w_down, norm_final_g):
    batch, seq, d = x.shape
    assert (seq, d) == (SEQ, D_MODEL)
    x2 = x.reshape(batch * seq, d)
    bs = jnp.broadcast_to(gmlp_b[:, :, None], (N_HEADS, CHUNK, GROUP_DIM))

    a, yg, w_up_b, w_down_b, w_out_b = _in_proj(
        x2, norm_mix_g.reshape(1, d), w_in, gmlp_v_g, gmlp_ws, bs,
        w_up, w_down, w_out, jnp.broadcast_to(norm_mlp_g[:, None], (d, GROUP_DIM)))
    yf = _fourier(a, fourier_w, batch)
    h, hb = _out_proj(yf, yg, x2, w_out_b)
    y = _mlp(hb, h, w_up_b, w_down_b, norm_final_g.reshape(1, d))
    return y.reshape(batch, seq, d)
```

```python
import numpy as np
import jax
import jax.numpy as jnp
from jax import lax
from jax.experimental import pallas as pl
from jax.experimental.pallas import tpu as pltpu

D_MODEL = 2048
SEQ = 2048
GROUP_DIM = 128
FOURIER_WIDTH = 1024
GMLP_WIDTH = 1024
N_GROUPS = FOURIER_WIDTH // GROUP_DIM
N_HEADS = GMLP_WIDTH // GROUP_DIM
IN_PROJ_WIDTH = FOURIER_WIDTH + 2 * GMLP_WIDTH
CHUNK = 128
D_FF = 4 * D_MODEL
EPS = 1e-6

F32 = jnp.float32
BF16 = jnp.bfloat16

VMEM_LIMIT_BYTES = 56 * 1024 * 1024
VMEM_LIMIT_OUT_PROJ_BYTES = 58 * 1024 * 1024

TM_IN = 512
STAGE_COLS = 512
HEADS_PER_DOT = 4
RADIX = 4
TM_OUT = 1024
SUB_OUT = 512
TM_MLP = 1024
TF_MLP = 1024
SUB_MLP = 512


def _dft_tables(n):
    k = np.arange(n, dtype=np.int64)
    ang = 2.0 * np.pi * ((k[:, None] * k[None, :]) % n).astype(np.float64) / n
    return np.cos(ang).astype(np.float32), np.sin(ang).astype(np.float32)


def _dot(a, b):
    return jnp.dot(a, b, preferred_element_type=F32)


def _rms_scale(v):
    return lax.rsqrt(jnp.mean(v * v, axis=-1, keepdims=True) + EPS)


def _fold_fourier_weights(cc_ref, sc_ref, wf_ref, m_ref):
    scale = 1.0 / np.sqrt(float(SEQ * GROUP_DIM))
    for g in range(N_GROUPS):
        w = wf_ref[g]
        m1 = jnp.dot(cc_ref[...], w, preferred_element_type=F32, precision=lax.Precision.HIGHEST)
        m2 = jnp.dot(sc_ref[...], w, preferred_element_type=F32, precision=lax.Precision.HIGHEST)
        m_ref[g, :GROUP_DIM, :] = (m1 * scale).astype(BF16)
        m_ref[g, GROUP_DIM:, :] = (m2 * (-scale)).astype(BF16)


def _gelu(z):
    return 0.5 * z * (1.0 + lax.erf(z * np.float32(np.sqrt(0.5))))


def _in_proj_kernel(x_ref, g_ref, w_hbm, gv_ref, ws_ref, bs_ref, wu_ref, wd_ref, wo_ref, gmlp_ref,
                    a_ref, yg_ref, wu_o_ref, wd_o_ref, wo_o_ref, pf_ref, w_ref, stage_ref, w_sem):
    n_stage = IN_PROJ_WIDTH // STAGE_COLS

    def stage_copy(k):
        return pltpu.make_async_copy(w_hbm.at[:, pl.ds(k * STAGE_COLS, STAGE_COLS)],
                                     stage_ref.at[k % 2], w_sem.at[k % 2])

    @pl.when(pl.program_id(0) == 0)
    def _():
        stage_copy(0).start()
        for k in range(n_stage):
            if k + 1 < n_stage:
                stage_copy(k + 1).start(priority=(k + 1) % 2)
            stage_copy(k).wait()
            w_ref[:, k * STAGE_COLS:(k + 1) * STAGE_COLS] = stage_ref[k % 2].astype(BF16)

    x = x_ref[...]
    xn = (x * _rms_scale(x) * g_ref[...]).astype(BF16)
    n_chunks = x.shape[0] // CHUNK
    width = HEADS_PER_DOT * GROUP_DIM
    n_blocks = N_HEADS // HEADS_PER_DOT

    def proj_u(hb):
        cu = FOURIER_WIDTH + hb * width
        return _gelu(_dot(xn, w_ref[:, cu:cu + width]))

    def proj_v(hb):
        cv = FOURIER_WIDTH + GMLP_WIDTH + hb * width
        return _gelu(_dot(xn, w_ref[:, cv:cv + width]))

    def gate(hb, u, v):
        for j in range(HEADS_PER_DOT):
            h = hb * HEADS_PER_DOT + j
            cols = slice(j * GROUP_DIM, (j + 1) * GROUP_DIM)
            vh = v[:, cols]
            vn = (vh * _rms_scale(vh) * gv_ref[h:h + 1, :]).astype(BF16)
            vcat = jnp.concatenate(
                [vn[c * CHUNK:(c + 1) * CHUNK, :] for c in range(n_chunks)], axis=1)
            s = _dot(ws_ref[h].astype(BF16), vcat)
            for c in range(n_chunks):
                rows = slice(c * CHUNK, (c + 1) * CHUNK)
                sc = s[:, c * GROUP_DIM:(c + 1) * GROUP_DIM] + bs_ref[h]
                yg_ref[rows, h * GROUP_DIM:(h + 1) * GROUP_DIM] = (u[rows, cols] * sc).astype(BF16)

    def fourier_cols(part, n_parts):
        groups = N_GROUPS // n_parts
        c0 = part * groups * GROUP_DIM
        pf = _dot(xn, w_ref[:, c0:c0 + groups * GROUP_DIM])
        for gl in range(groups):
            pf_ref[part * groups + gl] = pf[:, gl * GROUP_DIM:(gl + 1) * GROUP_DIM]
        for r in range(RADIX):
            for gl in range(groups):
                g = part * groups + gl
                a_ref[0, r, :, g * GROUP_DIM:(g + 1) * GROUP_DIM] = (
                    pf_ref[g, pl.ds(r, x.shape[0] // RADIX, stride=RADIX), :].astype(BF16))

    assert n_blocks == 2
    u0, v0 = proj_u(0), proj_v(0)
    u1 = proj_u(1)
    gate(0, u0, v0)
    v1 = proj_v(1)
    for c in range(wu_ref.shape[1] // GROUP_DIM):
        cols = slice(c * GROUP_DIM, (c + 1) * GROUP_DIM)
        wu_o_ref[0, :, cols] = (wu_ref[:, cols] * gmlp_ref[...]).astype(BF16)
    wd_o_ref[...] = wd_ref[...].astype(BF16)
    wo_o_ref[...] = wo_ref[...].astype(BF16)
    fourier_cols(0, 2)
    gate(1, u1, v1)
    fourier_cols(1, 2)


def _in_proj(x2, norm_g, w_in, gv, ws, bs, w_up, w_down, w_out, gmlp):
    rows = x2.shape[0]
    steps = rows // TM_IN
    ff_cols = D_FF // steps
    out_rows = D_MODEL // steps
    per_tile = TF_MLP // ff_cols
    assert ff_cols * steps == D_FF and out_rows * steps == D_MODEL and per_tile * ff_cols == TF_MLP
    tiles_per_seq = SEQ // TM_IN
    res_rows = TM_IN // RADIX
    const = lambda *shape: pl.BlockSpec(shape, lambda i: (0,) * len(shape),
                                        pipeline_mode=pl.Buffered(1))
    return pl.pallas_call(
        _in_proj_kernel,
        grid=(steps,),
        in_specs=[
            pl.BlockSpec((TM_IN, D_MODEL), lambda i: (i, 0)),
            const(1, D_MODEL),
            pl.BlockSpec(memory_space=pl.ANY),
            const(N_HEADS, GROUP_DIM),
            const(N_HEADS, CHUNK, CHUNK),
            const(N_HEADS, CHUNK, GROUP_DIM),
            pl.BlockSpec((D_MODEL, ff_cols), lambda i: (0, i)),
            pl.BlockSpec((ff_cols, D_MODEL), lambda i: (i, 0)),
            pl.BlockSpec((out_rows, D_MODEL), lambda i: (i, 0)),
            const(D_MODEL, GROUP_DIM),
        ],
        out_specs=[
            pl.BlockSpec((1, RADIX, res_rows, FOURIER_WIDTH),
                         lambda i: (i // tiles_per_seq, 0, i % tiles_per_seq, 0)),
            pl.BlockSpec((TM_IN, GMLP_WIDTH), lambda i: (i, 0)),
            pl.BlockSpec((1, D_MODEL, ff_cols), lambda i: (i // per_tile, 0, i % per_tile)),
            pl.BlockSpec((ff_cols, D_MODEL), lambda i: (i, 0)),
            pl.BlockSpec((out_rows, D_MODEL), lambda i: (i, 0)),
        ],
        out_shape=[
            jax.ShapeDtypeStruct((rows // SEQ, RADIX, SEQ // RADIX, FOURIER_WIDTH), BF16),
            jax.ShapeDtypeStruct((rows, GMLP_WIDTH), BF16),
            jax.ShapeDtypeStruct((D_FF // TF_MLP, D_MODEL, TF_MLP), BF16),
            jax.ShapeDtypeStruct((D_FF, D_MODEL), BF16),
            jax.ShapeDtypeStruct((D_MODEL, D_MODEL), BF16),
        ],
        scratch_shapes=[pltpu.VMEM((N_GROUPS, TM_IN, GROUP_DIM), F32),
                        pltpu.VMEM((D_MODEL, IN_PROJ_WIDTH), BF16),
                        pltpu.VMEM((2, D_MODEL, STAGE_COLS), F32),
                        pltpu.SemaphoreType.DMA((2,))],
        compiler_params=pltpu.CompilerParams(
            dimension_semantics=("arbitrary",), vmem_limit_bytes=VMEM_LIMIT_BYTES),
        name="in_proj_gmlp",
    )(x2, norm_g, w_in, gv, ws, bs, w_up, w_down, w_out, gmlp)


def _radix_tables():
    q = SEQ // RADIX
    kp = np.arange(q, dtype=np.int64)[:, None]
    m = np.arange(q, dtype=np.int64)[None, :]
    out = np.empty((RADIX, 2, q, q), np.float32)
    for r in range(RADIX):
        ang = 2.0 * np.pi * ((kp * (RADIX * m + r)) % SEQ).astype(np.float64) / SEQ
        out[r, 0] = np.cos(ang)
        out[r, 1] = np.sin(ang)
    return out


def _fourier_kernel(t_ref, a_ref, cc_ref, sc_ref, wf_ref, yf_ref, m_ref):
    @pl.when(pl.program_id(0) == 0)
    def _():
        _fold_fourier_weights(cc_ref, sc_ref, wf_ref, m_ref)

    q = SEQ // RADIX
    pc = [_dot(t_ref[r, 0], a_ref[0, r]) for r in range(RADIX)]
    ps = [_dot(t_ref[r, 1], a_ref[0, r]) for r in range(RADIX)]
    ac, bc, cc, dc = pc[0] + pc[2], pc[0] - pc[2], pc[1] + pc[3], pc[1] - pc[3]
    as_, bs, cs, ds = ps[0] + ps[2], ps[0] - ps[2], ps[1] + ps[3], ps[1] - ps[3]
    out_c = (ac + cc, bc - ds, ac - cc, bc + ds)
    out_s = (as_ + cs, bs + dc, as_ - cs, bs - dc)
    for j in range(RADIX):
        c = out_c[j].astype(BF16)
        s = out_s[j].astype(BF16)
        for g in range(m_ref.shape[0]):
            cols = slice(g * GROUP_DIM, (g + 1) * GROUP_DIM)
            y = _dot(jnp.concatenate([c[:, cols], s[:, cols]], axis=1), m_ref[g])
            yf_ref[j * q:(j + 1) * q, cols] = y.astype(BF16)


def _fourier(a, fourier_w, batch):
    assert RADIX == 4
    q = SEQ // RADIX
    tables = jnp.asarray(_radix_tables()).astype(BF16)
    cc, sc = _dft_tables(GROUP_DIM)
    const = lambda *shape: pl.BlockSpec(shape, lambda b: (0,) * len(shape),
                                        pipeline_mode=pl.Buffered(1))
    return pl.pallas_call(
        _fourier_kernel,
        grid=(batch,),
        in_specs=[
            const(RADIX, 2, q, q),
            pl.BlockSpec((1, RADIX, q, FOURIER_WIDTH), lambda b: (b, 0, 0, 0)),
            const(GROUP_DIM, GROUP_DIM),
            const(GROUP_DIM, GROUP_DIM),
            const(N_GROUPS, GROUP_DIM, GROUP_DIM),
        ],
        out_specs=pl.BlockSpec((SEQ, FOURIER_WIDTH), lambda b: (b, 0)),
        out_shape=jax.ShapeDtypeStruct((batch * SEQ, FOURIER_WIDTH), BF16),
        scratch_shapes=[pltpu.VMEM((N_GROUPS, 2 * GROUP_DIM, GROUP_DIM), BF16)],
        compiler_params=pltpu.CompilerParams(
            dimension_semantics=("arbitrary",), vmem_limit_bytes=VMEM_LIMIT_BYTES),
        name="fourier_mix",
    )(tables, a, jnp.asarray(cc), jnp.asarray(sc), fourier_w)


def _out_proj_kernel(yf_ref, yg_ref, x_ref, w_ref, h_ref, hb_ref):
    for rb in range(TM_OUT // SUB_OUT):
        rows = slice(rb * SUB_OUT, (rb + 1) * SUB_OUT)
        mix = jnp.concatenate([yf_ref[rows, :], yg_ref[rows, :]], axis=1)
        h = x_ref[rows, :] + _dot(mix, w_ref[...])
        h_ref[rows, :] = h
        hb_ref[rows, :] = h.astype(BF16)


def _out_proj(yf, yg, x2, w_out):
    rows = x2.shape[0]
    return pl.pallas_call(
        _out_proj_kernel,
        grid=(rows // TM_OUT,),
        in_specs=[
            pl.BlockSpec((TM_OUT, FOURIER_WIDTH), lambda i: (i, 0)),
            pl.BlockSpec((TM_OUT, GMLP_WIDTH), lambda i: (i, 0)),
            pl.BlockSpec((TM_OUT, D_MODEL), lambda i: (i, 0)),
            pl.BlockSpec((D_MODEL, D_MODEL), lambda i: (0, 0), pipeline_mode=pl.Buffered(1)),
        ],
        out_specs=[
            pl.BlockSpec((TM_OUT, D_MODEL), lambda i: (i, 0)),
            pl.BlockSpec((TM_OUT, D_MODEL), lambda i: (i, 0)),
        ],
        out_shape=[
            jax.ShapeDtypeStruct((rows, D_MODEL), F32),
            jax.ShapeDtypeStruct((rows, D_MODEL), BF16),
        ],
        compiler_params=pltpu.CompilerParams(
            dimension_semantics=("arbitrary",), vmem_limit_bytes=VMEM_LIMIT_OUT_PROJ_BYTES),
        name="out_proj",
    )(yf, yg, x2, w_out)


def _mlp_kernel(hb_ref, h_hbm, wu_ref, wd_ref, g_ref, o_ref, h_buf, h_sem):
    i = pl.program_id(0)
    f = pl.program_id(1)
    last = pl.num_programs(1) - 1

    def h_copy():
        return pltpu.make_async_copy(h_hbm.at[pl.ds(i * TM_MLP, TM_MLP), :], h_buf, h_sem)

    def mlp_rows(rb):
        rows = slice(rb * SUB_MLP, (rb + 1) * SUB_MLP)
        r = jnp.maximum(_dot(hb_ref[rows, :], wu_ref[0]), 0.0)
        return rows, _dot((r * r).astype(BF16), wd_ref[...])

    @pl.when(f == 0)
    def _():
        for rb in range(TM_MLP // SUB_MLP):
            rows, d = mlp_rows(rb)
            o_ref[rows, :] = d
        h_copy().start(priority=1)

    @pl.when(jnp.logical_and(f > 0, f < last))
    def _():
        for rb in range(TM_MLP // SUB_MLP):
            rows, d = mlp_rows(rb)
            o_ref[rows, :] += d

    @pl.when(f == last)
    def _():
        h_copy().wait()
        for rb in range(TM_MLP // SUB_MLP):
            rows, d = mlp_rows(rb)
            h = h_buf[rows, :]
            s = _rms_scale(h)
            t = h + (o_ref[rows, :] + d) * (s * s)
            o_ref[rows, :] = t * _rms_scale(t) * g_ref[...]


def _mlp(hb, h, w_up, w_down, norm_g):
    rows = hb.shape[0]
    assert D_FF // TF_MLP >= 2
    return pl.pallas_call(
        _mlp_kernel,
        grid=(rows // TM_MLP, D_FF // TF_MLP),
        in_specs=[
            pl.BlockSpec((TM_MLP, D_MODEL), lambda i, f: (i, 0)),
            pl.BlockSpec(memory_space=pl.ANY),
            pl.BlockSpec((1, D_MODEL, TF_MLP), lambda i, f: (f, 0, 0)),
            pl.BlockSpec((TF_MLP, D_MODEL), lambda i, f: (f, 0)),
            pl.BlockSpec((1, D_MODEL), lambda i, f: (0, 0), pipeline_mode=pl.Buffered(1)),
        ],
        out_specs=pl.BlockSpec((TM_MLP, D_MODEL), lambda i, f: (i, 0)),
        out_shape=jax.ShapeDtypeStruct((rows, D_MODEL), F32),
        scratch_shapes=[pltpu.VMEM((TM_MLP, D_MODEL), F32), pltpu.SemaphoreType.DMA(())],
        compiler_params=pltpu.CompilerParams(
            dimension_semantics=("arbitrary", "arbitrary"), vmem_limit_bytes=VMEM_LIMIT_BYTES),
        name="mlp",
    )(hb, h, w_up, w_down, norm_g)


def kernel(x, norm_mix_g, w_in, fourier_w, gmlp_v_g, gmlp_ws, gmlp_b, w_out,
           norm_mlp_g, w_up, w_down, norm_final_g):
    batch, seq, d = x.shape
    assert (seq, d) == (SEQ, D_MODEL)
    x2 = x.reshape(batch * seq, d)
    bs = jnp.broadcast_to(gmlp_b[:, :, None], (N_HEADS, CHUNK, GROUP_DIM))

    a, yg, w_up_b, w_down_b, w_out_b = _in_proj(
        x2, norm_mix_g.reshape(1, d), w_in, gmlp_v_g, gmlp_ws, bs,
        w_up, w_down, w_out, jnp.broadcast_to(norm_mlp_g[:, None], (d, GROUP_DIM)))
    yf = _fourier(a, fourier_w, batch)
    h, hb = _out_proj(yf, yg, x2, w_out_b)
    y = _mlp(hb, h, w_up_b, w_down_b, norm_final_g.reshape(1, d))
    return y.reshape(batch, seq, d)
```

```python
import numpy as np
import jax
import jax.numpy as jnp
from jax import lax
from jax.experimental import pallas as pl
from jax.experimental.pallas import tpu as pltpu

D_MODEL = 2048
SEQ = 2048
GROUP_DIM = 128
FOURIER_WIDTH = 1024
GMLP_WIDTH = 1024
N_GROUPS = FOURIER_WIDTH // GROUP_DIM
N_HEADS = GMLP_WIDTH // GROUP_DIM
IN_PROJ_WIDTH = FOURIER_WIDTH + 2 * GMLP_WIDTH
CHUNK = 128
D_FF = 4 * D_MODEL
EPS = 1e-6

F32 = jnp.float32
BF16 = jnp.bfloat16

VMEM_LIMIT_BYTES = 56 * 1024 * 1024
VMEM_LIMIT_OUT_PROJ_BYTES = 58 * 1024 * 1024

TM_IN = 512
STAGE_COLS = 512
HEADS_PER_DOT = 4
RADIX = 4
TM_OUT = 1024
SUB_OUT = 512
TM_MLP = 1024
TF_MLP = 1024
SUB_MLP = 512


def _dft_tables(n):
    k = np.arange(n, dtype=np.int64)
    ang = 2.0 * np.pi * ((k[:, None] * k[None, :]) % n).astype(np.float64) / n
    return np.cos(ang).astype(np.float32), np.sin(ang).astype(np.float32)


def _dot(a, b):
    return jnp.dot(a, b, preferred_element_type=F32)


def _rms_scale(v):
    return lax.rsqrt(jnp.mean(v * v, axis=-1, keepdims=True) + EPS)


def _fold_fourier_weights(cc_ref, sc_ref, wf_ref, m_ref):
    scale = 1.0 / np.sqrt(float(SEQ * GROUP_DIM))
    for g in range(N_GROUPS):
        w = wf_ref[g]
        m1 = jnp.dot(cc_ref[...], w, preferred_element_type=F32, precision=lax.Precision.HIGHEST)
        m2 = jnp.dot(sc_ref[...], w, preferred_element_type=F32, precision=lax.Precision.HIGHEST)
        m_ref[g, :GROUP_DIM, :] = (m1 * scale).astype(BF16)
        m_ref[g, GROUP_DIM:, :] = (m2 * (-scale)).astype(BF16)


def _gelu(z):
    return 0.5 * z * (1.0 + lax.erf(z * np.float32(np.sqrt(0.5))))


def _in_proj_kernel(x_ref, g_ref, w_hbm, gv_ref, ws_ref, bs_ref, wu_ref, wd_ref, wo_ref, gmlp_ref,
                    a_ref, yg_ref, wu_o_ref, wd_o_ref, wo_o_ref, pf_ref, w_ref, stage_ref, w_sem):
    n_stage = IN_PROJ_WIDTH // STAGE_COLS

    def stage_copy(k):
        return pltpu.make_async_copy(w_hbm.at[:, pl.ds(k * STAGE_COLS, STAGE_COLS)],
                                     stage_ref.at[k % 2], w_sem.at[k % 2])

    @pl.when(pl.program_id(0) == 0)
    def _():
        stage_copy(0).start()
        for k in range(n_stage):
            if k + 1 < n_stage:
                stage_copy(k + 1).start(priority=(k + 1) % 2)
            stage_copy(k).wait()
            w_ref[:, k * STAGE_COLS:(k + 1) * STAGE_COLS] = stage_ref[k % 2].astype(BF16)

    x = x_ref[...]
    xn = (x * _rms_scale(x) * g_ref[...]).astype(BF16)
    n_chunks = x.shape[0] // CHUNK
    width = HEADS_PER_DOT * GROUP_DIM
    n_blocks = N_HEADS // HEADS_PER_DOT

    def proj_u(hb):
        cu = FOURIER_WIDTH + hb * width
        return _gelu(_dot(xn, w_ref[:, cu:cu + width]))

    def proj_v(hb):
        cv = FOURIER_WIDTH + GMLP_WIDTH + hb * width
        return _gelu(_dot(xn, w_ref[:, cv:cv + width]))

    def gate(hb, u, v):
        for j in range(HEADS_PER_DOT):
            h = hb * HEADS_PER_DOT + j
            cols = slice(j * GROUP_DIM, (j + 1) * GROUP_DIM)
            vh = v[:, cols]
            vn = (vh * _rms_scale(vh) * gv_ref[h:h + 1, :]).astype(BF16)
            vcat = jnp.concatenate(
                [vn[c * CHUNK:(c + 1) * CHUNK, :] for c in range(n_chunks)], axis=1)
            s = _dot(ws_ref[h].astype(BF16), vcat)
            for c in range(n_chunks):
                rows = slice(c * CHUNK, (c + 1) * CHUNK)
                sc = s[:, c * GROUP_DIM:(c + 1) * GROUP_DIM] + bs_ref[h]
                yg_ref[rows, h * GROUP_DIM:(h + 1) * GROUP_DIM] = (u[rows, cols] * sc).astype(BF16)

    def fourier_cols(part, n_parts):
        groups = N_GROUPS // n_parts
        c0 = part * groups * GROUP_DIM
        pf = _dot(xn, w_ref[:, c0:c0 + groups * GROUP_DIM])
        for gl in range(groups):
            pf_ref[part * groups + gl] = pf[:, gl * GROUP_DIM:(gl + 1) * GROUP_DIM]
        for r in range(RADIX):
            for gl in range(groups):
                g = part * groups + gl
                a_ref[0, r, :, g * GROUP_DIM:(g + 1) * GROUP_DIM] = (
                    pf_ref[g, pl.ds(r, x.shape[0] // RADIX, stride=RADIX), :].astype(BF16))

    assert n_blocks == 2
    u0, v0 = proj_u(0), proj_v(0)
    u1 = proj_u(1)
    gate(0, u0, v0)
    v1 = proj_v(1)
    for c in range(wu_ref.shape[1] // GROUP_DIM):
        cols = slice(c * GROUP_DIM, (c + 1) * GROUP_DIM)
        wu_o_ref[0, :, cols] = (wu_ref[:, cols] * gmlp_ref[...]).astype(BF16)
    wd_o_ref[...] = wd_ref[...].astype(BF16)
    wo_o_ref[...] = wo_ref[...].astype(BF16)
    fourier_cols(0, 2)
    gate(1, u1, v1)
    fourier_cols(1, 2)


def _in_proj(x2, norm_g, w_in, gv, ws, bs, w_up, w_down, w_out, gmlp):
    rows = x2.shape[0]
    steps = rows // TM_IN
    ff_cols = D_FF // steps
    out_rows = D_MODEL // steps
    per_tile = TF_MLP // ff_cols
    assert ff_cols * steps == D_FF and out_rows * steps == D_MODEL and per_tile * ff_cols == TF_MLP
    tiles_per_seq = SEQ // TM_IN
    res_rows = TM_IN // RADIX
    const = lambda *shape: pl.BlockSpec(shape, lambda i: (0,) * len(shape),
                                        pipeline_mode=pl.Buffered(1))
    return pl.pallas_call(
        _in_proj_kernel,
        grid=(steps,),
        in_specs=[
            pl.BlockSpec((TM_IN, D_MODEL), lambda i: (i, 0)),
            const(1, D_MODEL),
            pl.BlockSpec(memory_space=pl.ANY),
            const(N_HEADS, GROUP_DIM),
            const(N_HEADS, CHUNK, CHUNK),
            const(N_HEADS, CHUNK, GROUP_DIM),
            pl.BlockSpec((D_MODEL, ff_cols), lambda i: (0, i)),
            pl.BlockSpec((ff_cols, D_MODEL), lambda i: (i, 0)),
            pl.BlockSpec((out_rows, D_MODEL), lambda i: (i, 0)),
            const(D_MODEL, GROUP_DIM),
        ],
        out_specs=[
            pl.BlockSpec((1, RADIX, res_rows, FOURIER_WIDTH),
                         lambda i: (i // tiles_per_seq, 0, i % tiles_per_seq, 0)),
            pl.BlockSpec((TM_IN, GMLP_WIDTH), lambda i: (i, 0)),
            pl.BlockSpec((1, D_MODEL, ff_cols), lambda i: (i // per_tile, 0, i % per_tile)),
            pl.BlockSpec((ff_cols, D_MODEL), lambda i: (i, 0)),
            pl.BlockSpec((out_rows, D_MODEL), lambda i: (i, 0)),
        ],
        out_shape=[
            jax.ShapeDtypeStruct((rows // SEQ, RADIX, SEQ // RADIX, FOURIER_WIDTH), BF16),
            jax.ShapeDtypeStruct((rows, GMLP_WIDTH), BF16),
            jax.ShapeDtypeStruct((D_FF // TF_MLP, D_MODEL, TF_MLP), BF16),
            jax.ShapeDtypeStruct((D_FF, D_MODEL), BF16),
            jax.ShapeDtypeStruct((D_MODEL, D_MODEL), BF16),
        ],
        scratch_shapes=[pltpu.VMEM((N_GROUPS, TM_IN, GROUP_DIM), F32),
                        pltpu.VMEM((D_MODEL, IN_PROJ_WIDTH), BF16),
                        pltpu.VMEM((2, D_MODEL, STAGE_COLS), F32),
                        pltpu.SemaphoreType.DMA((2,))],
        compiler_params=pltpu.CompilerParams(
            dimension_semantics=("arbitrary",), vmem_limit_bytes=VMEM_LIMIT_BYTES),
        name="in_proj_gmlp",
    )(x2, norm_g, w_in, gv, ws, bs, w_up, w_down, w_out, gmlp)


def _radix_tables():
    q = SEQ // RADIX
    kp = np.arange(q, dtype=np.int64)[:, None]
    m = np.arange(q, dtype=np.int64)[None, :]
    out = np.empty((RADIX, 2, q, q), np.float32)
    for r in range(RADIX):
        ang = 2.0 * np.pi * ((kp * (RADIX * m + r)) % SEQ).astype(np.float64) / SEQ
        out[r, 0] = np.cos(ang)
        out[r, 1] = np.sin(ang)
    return out


def _fourier_kernel(t_ref, a_ref, cc_ref, sc_ref, wf_ref, yf_ref, m_ref):
    @pl.when(pl.program_id(0) == 0)
    def _():
        _fold_fourier_weights(cc_ref, sc_ref, wf_ref, m_ref)

    q = SEQ // RADIX
    pc = [_dot(t_ref[r, 0], a_ref[0, r]) for r in range(RADIX)]
    ps = [_dot(t_ref[r, 1], a_ref[0, r]) for r in range(RADIX)]
    ac, bc, cc, dc = pc[0] + pc[2], pc[0] - pc[2], pc[1] + pc[3], pc[1] - pc[3]
    as_, bs, cs, ds = ps[0] + ps[2], ps[0] - ps[2], ps[1] + ps[3], ps[1] - ps[3]
    out_c = (ac + cc, bc - ds, ac - cc, bc + ds)
    out_s = (as_ + cs, bs + dc, as_ - cs, bs - dc)
    for j in range(RADIX):
        c = out_c[j].astype(BF16)
        s = out_s[j].astype(BF16)
        for g in range(m_ref.shape[0]):
            cols = slice(g * GROUP_DIM, (g + 1) * GROUP_DIM)
            y = _dot(jnp.concatenate([c[:, cols], s[:, cols]], axis=1), m_ref[g])
            yf_ref[j * q:(j + 1) * q, cols] = y.astype(BF16)


def _fourier(a, fourier_w, batch):
    assert RADIX == 4
    q = SEQ // RADIX
    tables = jnp.asarray(_radix_tables()).astype(BF16)
    cc, sc = _dft_tables(GROUP_DIM)
    const = lambda *shape: pl.BlockSpec(shape, lambda b: (0,) * len(shape),
                                        pipeline_mode=pl.Buffered(1))
    return pl.pallas_call(
        _fourier_kernel,
        grid=(batch,),
        in_specs=[
            const(RADIX, 2, q, q),
            pl.BlockSpec((1, RADIX, q, FOURIER_WIDTH), lambda b: (b, 0, 0, 0)),
            const(GROUP_DIM, GROUP_DIM),
            const(GROUP_DIM, GROUP_DIM),
            const(N_GROUPS, GROUP_DIM, GROUP_DIM),
        ],
        out_specs=pl.BlockSpec((SEQ, FOURIER_WIDTH), lambda b: (b, 0)),
        out_shape=jax.ShapeDtypeStruct((batch * SEQ, FOURIER_WIDTH), BF16),
        scratch_shapes=[pltpu.VMEM((N_GROUPS, 2 * GROUP_DIM, GROUP_DIM), BF16)],
        compiler_params=pltpu.CompilerParams(
            dimension_semantics=("arbitrary",), vmem_limit_bytes=VMEM_LIMIT_BYTES),
        name="fourier_mix",
    )(tables, a, jnp.asarray(cc), jnp.asarray(sc), fourier_w)


def _out_proj_kernel(yf_ref, yg_ref, x_ref, w_ref, h_ref, hb_ref):
    for rb in range(TM_OUT // SUB_OUT):
        rows = slice(rb * SUB_OUT, (rb + 1) * SUB_OUT)
        mix = jnp.concatenate([yf_ref[rows, :], yg_ref[rows, :]], axis=1)
        h = x_ref[rows, :] + _dot(mix, w_ref[...])
        h_ref[rows, :] = h
        hb_ref[rows, :] = h.astype(BF16)


def _out_proj(yf, yg, x2, w_out):
    rows = x2.shape[0]
    return pl.pallas_call(
        _out_proj_kernel,
        grid=(rows // TM_OUT,),
        in_specs=[
            pl.BlockSpec((TM_OUT, FOURIER_WIDTH), lambda i: (i, 0)),
            pl.BlockSpec((TM_OUT, GMLP_WIDTH), lambda i: (i, 0)),
            pl.BlockSpec((TM_OUT, D_MODEL), lambda i: (i, 0)),
            pl.BlockSpec((D_MODEL, D_MODEL), lambda i: (0, 0), pipeline_mode=pl.Buffered(1)),
        ],
        out_specs=[
            pl.BlockSpec((TM_OUT, D_MODEL), lambda i: (i, 0)),
            pl.BlockSpec((TM_OUT, D_MODEL), lambda i: (i, 0)),
        ],
        out_shape=[
            jax.ShapeDtypeStruct((rows, D_MODEL), F32),
            jax.ShapeDtypeStruct((rows, D_MODEL), BF16),
        ],
        compiler_params=pltpu.CompilerParams(
            dimension_semantics=("arbitrary",), vmem_limit_bytes=VMEM_LIMIT_OUT_PROJ_BYTES),
        name="out_proj",
    )(yf, yg, x2, w_out)


def _mlp_kernel(hb_ref, h_hbm, wu_ref, wd_ref, g_ref, o_ref, h_buf, h_sem):
    i = pl.program_id(0)
    f = pl.program_id(1)
    last = pl.num_programs(1) - 1

    def h_copy():
        return pltpu.make_async_copy(h_hbm.at[pl.ds(i * TM_MLP, TM_MLP), :], h_buf, h_sem)

    def mlp_rows(rb):
        rows = slice(rb * SUB_MLP, (rb + 1) * SUB_MLP)
        r = jnp.maximum(_dot(hb_ref[rows, :], wu_ref[0]), 0.0)
        return rows, _dot((r * r).astype(BF16), wd_ref[...])

    @pl.when(f == 0)
    def _():
        for rb in range(TM_MLP // SUB_MLP):
            rows, d = mlp_rows(rb)
            o_ref[rows, :] = d
        h_copy().start(priority=1)

    @pl.when(jnp.logical_and(f > 0, f < last))
    def _():
        for rb in range(TM_MLP // SUB_MLP):
            rows, d = mlp_rows(rb)
            o_ref[rows, :] += d

    @pl.when(f == last)
    def _():
        h_copy().wait()
        for rb in range(TM_MLP // SUB_MLP):
            rows, d = mlp_rows(rb)
            h = h_buf[rows, :]
            s = _rms_scale(h)
            t = h + (o_ref[rows, :] + d) * (s * s)
            o_ref[rows, :] = t * _rms_scale(t) * g_ref[...]


def _mlp(hb, h, w_up, w_down, norm_g):
    rows = hb.shape[0]
    n_hidden = D_FF // TF_MLP
    assert n_hidden >= 2
    tile = lambda i, f: jnp.where(i % 2 == 0, f, n_hidden - 1 - f)
    return pl.pallas_call(
        _mlp_kernel,
        grid=(rows // TM_MLP, n_hidden),
        in_specs=[
            pl.BlockSpec((TM_MLP, D_MODEL), lambda i, f: (i, 0)),
            pl.BlockSpec(memory_space=pl.ANY),
            pl.BlockSpec((1, D_MODEL, TF_MLP), lambda i, f: (tile(i, f), 0, 0)),
            pl.BlockSpec((TF_MLP, D_MODEL), lambda i, f: (tile(i, f), 0)),
            pl.BlockSpec((1, D_MODEL), lambda i, f: (0, 0), pipeline_mode=pl.Buffered(1)),
        ],
        out_specs=pl.BlockSpec((TM_MLP, D_MODEL), lambda i, f: (i, 0)),
        out_shape=jax.ShapeDtypeStruct((rows, D_MODEL), F32),
        scratch_shapes=[pltpu.VMEM((TM_MLP, D_MODEL), F32), pltpu.SemaphoreType.DMA(())],
        compiler_params=pltpu.CompilerParams(
            dimension_semantics=("arbitrary", "arbitrary"), vmem_limit_bytes=VMEM_LIMIT_BYTES),
        name="mlp",
    )(hb, h, w_up, w_down, norm_g)


def kernel(x, norm_mix_g, w_in, fourier_w, gmlp_v_g, gmlp_ws, gmlp_b, w_out,
           norm_mlp_g, w_up, w_down, norm_final_g):
    batch, seq, d = x.shape
    assert (seq, d) == (SEQ, D_MODEL)
    x2 = x.reshape(batch * seq, d)
    bs = jnp.broadcast_to(gmlp_b[:, :, None], (N_HEADS, CHUNK, GROUP_DIM))

    a, yg, w_up_b, w_down_b, w_out_b = _in_proj(
        x2, norm_mix_g.reshape(1, d), w_in, gmlp_v_g, gmlp_ws, bs,
        w_up, w_down, w_out, jnp.broadcast_to(norm_mlp_g[:, None], (d, GROUP_DIM)))
    yf = _fourier(a, fourier_w, batch)
    h, hb = _out_proj(yf, yg, x2, w_out_b)
    y = _mlp(hb, h, w_up_b, w_down_b, norm_final_g.reshape(1, d))
    return y.reshape(batch, seq, d)
```
